```python
import math
import jax, jax.numpy as jnp
from jax import lax
import numpy as np

D_MODEL = 1024
BATCH = 4
SEQ = 8192
DEPTH = 2

N_HEADS = 8
HEAD_DIM = 64
ATTN_WIDTH = N_HEADS * HEAD_DIM
N_IDX_HEADS = 8
IDX_DIM = 64
TOPK_MAX = 256
Q_BLOCK = 128
CONV_CH = D_MODEL // 2
CONV_WIDTH = 31
D_FF = ((-(-8 * D_MODEL // 3)) + 255) // 256 * 256
N_BUCKETS = 32
MAX_DISTANCE = 128
EPS = 1e-6
SPLIT_SIZES = (ATTN_WIDTH, ATTN_WIDTH, ATTN_WIDTH,
               N_IDX_HEADS * IDX_DIM, IDX_DIM, N_IDX_HEADS,
               2 * CONV_CH,
               D_MODEL, D_MODEL)
N_IN = sum(SPLIT_SIZES)

kernel_name = "dsa_conformer_gated_hybrid"


def rmsnorm(x, g):
    xf = x.astype(jnp.float32)
    y = xf * lax.rsqrt(jnp.mean(xf * xf, axis=-1, keepdims=True) + EPS)
    return (y * g.astype(jnp.float32)).astype(x.dtype)


def layernorm(x, g, b):
    xf = x.astype(jnp.float32)
    mu = jnp.mean(xf, axis=-1, keepdims=True)
    var = jnp.mean(jnp.square(xf - mu), axis=-1, keepdims=True)
    y = (xf - mu) * lax.rsqrt(var + EPS)
    return (y * g.astype(jnp.float32) + b.astype(jnp.float32)).astype(x.dtype)


def t5_bucket(dist):
    n = jnp.maximum(dist, 0)
    max_exact = N_BUCKETS // 2
    nf = jnp.maximum(n, 1).astype(jnp.float32)
    large = max_exact + (jnp.log(nf / max_exact) / math.log(MAX_DISTANCE / max_exact)
                         * (N_BUCKETS - max_exact)).astype(jnp.int32)
    large = jnp.minimum(large, N_BUCKETS - 1)
    return jnp.where(n < max_exact, n, large)


def dsa_attention(q, k, v, q_idx, k_idx, w_idx, rel_bias):
    B, S = q.shape[0], q.shape[1]
    topk = min(TOPK_MAX, S // 4)
    nb = S // Q_BLOCK
    k_idx_f = k_idx.astype(jnp.float32)
    bias_tab = rel_bias.astype(jnp.float32)
    key_pos = jnp.arange(S, dtype=jnp.int32)
    gather = jax.vmap(lambda a, i: a[i])

    def to_blocks(a):
        return a.reshape((B, nb, Q_BLOCK) + a.shape[2:]).swapaxes(0, 1)

    def block(args):
        bi, qb, qib, wb = args
        t = bi * Q_BLOCK + jnp.arange(Q_BLOCK, dtype=jnp.int32)
        causal = key_pos[None, :] <= t[:, None]
        idx_logits = jnp.einsum('bqhd,bsd->bqhs', qib.astype(jnp.float32), k_idx_f) * (IDX_DIM ** -0.5)
        w = wb.astype(jnp.float32) * (N_IDX_HEADS ** -0.5)
        score = jnp.einsum('bqhs,bqh->bqs', jax.nn.relu(idx_logits), w)
        score = jnp.where(causal[None], score, -jnp.inf)
        _, sel = lax.top_k(score, topk)
        k_sel = gather(k, sel)
        v_sel = gather(v, sel)
        logits = jnp.einsum('bqhd,bqkhd->bhqk', qb, k_sel).astype(jnp.float32) * (HEAD_DIM ** -0.5)
        dist = t[None, :, None] - sel
        bias = bias_tab[t5_bucket(dist)]
        logits = logits + bias.transpose(0, 3, 1, 2)
        logits = jnp.where((dist >= 0)[:, None], logits, -jnp.inf)
        p = jax.nn.softmax(logits, axis=-1).astype(v.dtype)
        return jnp.einsum('bhqk,bqkhd->bqhd', p, v_sel)

    out = lax.map(block, (jnp.arange(nb, dtype=jnp.int32), to_blocks(q), to_blocks(q_idx), to_blocks(w_idx)))
    return out.swapaxes(0, 1).reshape(B, S, ATTN_WIDTH)


def conformer_conv(u, dw_kernel, dw_bias, g, b):
    a, gate = jnp.split(u, 2, axis=-1)
    h = a * jax.nn.sigmoid(gate)
    h = lax.conv_general_dilated(h, dw_kernel[:, None, :].astype(h.dtype), window_strides=(1,),
                                 padding=[(CONV_WIDTH - 1, 0)],
                                 dimension_numbers=('NWC', 'WIO', 'NWC'),
                                 feature_group_count=CONV_CH) + dw_bias.astype(h.dtype)
    h = layernorm(h, g, b)
    return jax.nn.silu(h)


def setup_inputs(seed: int = 0) -> dict:
    key = jax.random.key(seed)
    ks = jax.random.split(key, 16)
    f32 = jnp.float32

    def w(k, shape, fan_in):
        return jax.random.normal(k, shape, f32) * (fan_in ** -0.5)

    def gain(k, shape):
        return 1.0 + 0.05 * jax.random.normal(k, shape, f32)

    return {
        "x": jax.random.normal(ks[0], (BATCH, SEQ, D_MODEL), f32),
        "rel_bias": 0.5 * jax.random.normal(ks[1], (N_BUCKETS, N_HEADS), f32),
        "mix_norm": gain(ks[2], (DEPTH, D_MODEL)),
        "w_in": w(ks[3], (DEPTH, D_MODEL, N_IN), D_MODEL),
        "w_attn_out": w(ks[4], (DEPTH, ATTN_WIDTH, D_MODEL), ATTN_WIDTH),
        "dw_kernel": w(ks[5], (DEPTH, CONV_WIDTH, CONV_CH), CONV_WIDTH),
        "dw_bias": 0.02 * jax.random.normal(ks[6], (DEPTH, CONV_CH), f32),
        "conv_norm_g": gain(ks[7], (DEPTH, CONV_CH)),
        "conv_norm_b": 0.02 * jax.random.normal(ks[8], (DEPTH, CONV_CH), f32),
        "w_conv_out": w(ks[9], (DEPTH, CONV_CH, D_MODEL), CONV_CH),
        "w_mix_out": w(ks[10], (DEPTH, D_MODEL, D_MODEL), D_MODEL),
        "ffn_norm": gain(ks[11], (DEPTH, D_MODEL)),
        "w_ffn_in": w(ks[12], (DEPTH, D_MODEL, 2 * D_FF), D_MODEL),
        "w_ffn_out": w(ks[13], (DEPTH, D_FF, D_MODEL), D_FF),
        "final_norm": gain(ks[14], (D_MODEL,)),
    }


def reference(x, rel_bias, mix_norm, w_in, w_attn_out, dw_kernel, dw_bias, conv_norm_g, conv_norm_b,
              w_conv_out, w_mix_out, ffn_norm, w_ffn_in, w_ffn_out, final_norm):
    B, S, _ = x.shape
    split_points = list(np.cumsum(SPLIT_SIZES)[:-1])
    for l in range(DEPTH):
        h = rmsnorm(x, mix_norm[l])
        z = h @ w_in[l]
        q, k, v, qi, ki, wi, u_conv, g_a, g_b = jnp.split(z, split_points, axis=-1)
        q = q.reshape(B, S, N_HEADS, HEAD_DIM)
        k = k.reshape(B, S, N_HEADS, HEAD_DIM)
        v = v.reshape(B, S, N_HEADS, HEAD_DIM)
        qi = qi.reshape(B, S, N_IDX_HEADS, IDX_DIM)
        y_a = dsa_attention(q, k, v, qi, ki, wi, rel_bias) @ w_attn_out[l]
        y_b = conformer_conv(u_conv, dw_kernel[l], dw_bias[l], conv_norm_g[l], conv_norm_b[l]) @ w_conv_out[l]
        merged = jax.nn.sigmoid(g_a) * y_a + jax.nn.sigmoid(g_b) * y_b
        x = x + merged @ w_mix_out[l]
        h2 = rmsnorm(x, ffn_norm[l])
        gate, up = jnp.split(h2 @ w_ffn_in[l], 2, axis=-1)
        x = x + (jax.nn.silu(gate) * up) @ w_ffn_out[l]
    return rmsnorm(x, final_norm)
```

```python
import functools
import math

import numpy as np
import jax
import jax.numpy as jnp
from jax import lax
from jax.experimental import pallas as pl
from jax.experimental.pallas import tpu as pltpu

F32 = jnp.float32
BF16 = jnp.bfloat16
I32 = jnp.int32

N_HEADS = 8
HEAD_DIM = 64
ATTN_WIDTH = N_HEADS * HEAD_DIM
N_IDX_HEADS = 8
IDX_DIM = 64
TOPK_MAX = 256
CONV_WIDTH = 31
N_BUCKETS = 32
MAX_DISTANCE = 128
EPS = 1e-6

LANES = 128
SUBLANES = 8
VMEM_LIMIT_BYTES = 56 * 1024 * 1024
LOG2E = math.log2(math.e)
NEG_BIG = -1e30
INT_MIN = -(2 ** 31)

ATTN_TILE = 256
CONV_HALO = 32


def _sigmoid(x):
    return 1.0 / (1.0 + jnp.exp(-x))


def _t5_bucket_ranges():
    max_exact = N_BUCKETS // 2
    buckets = []
    for n in range(0, 4 * MAX_DISTANCE):
        if n < max_exact:
            buckets.append(n)
        else:
            b = max_exact + int(math.log(n / max_exact) / math.log(MAX_DISTANCE / max_exact)
                                * (N_BUCKETS - max_exact))
            buckets.append(min(b, N_BUCKETS - 1))
    ranges = []
    for b in range(N_BUCKETS - 1):
        ns = [n for n, bb in enumerate(buckets) if bb == b]
        ranges.append((min(ns), max(ns)))
    return ranges


_BUCKET_RANGES = _t5_bucket_ranges()
_BIAS_REACH = _BUCKET_RANGES[-1][1] + 1


def _inproj_kernel(x_ref, g_ref, wa_ref, wkw_ref, wu_ref, wg_ref,
                   q_ref, k_ref, v_ref, qi_ref, ki_ref, wi_ref, u_ref, gg_ref, *, q_scale):
    x = x_ref[...]
    ms = jnp.mean(x * x, axis=-1, keepdims=True)
    h = ((x * lax.rsqrt(ms + EPS)) * g_ref[...]).astype(BF16)

    def mm(w):
        return jnp.dot(h, w, preferred_element_type=F32)

    w = ATTN_WIDTH
    q_ref[...] = (mm(wa_ref[:, 0:w]) * q_scale).astype(BF16)
    k_ref[...] = mm(wa_ref[:, w:2 * w]).astype(BF16)
    v_ref[...] = mm(wa_ref[:, 2 * w:3 * w]).astype(BF16)
    qi_ref[...] = mm(wa_ref[:, 3 * w:4 * w]).astype(BF16)
    kw = mm(wkw_ref[...])
    ki_ref[...] = kw[:, :LANES].astype(BF16)
    wi_ref[...] = kw[:, LANES:]
    u_ref[...] = mm(wu_ref[...])
    gg_ref[...] = mm(wg_ref[...])


def _inproj(x2d, g, wa, wkw, wu, wg, *, tm):
    n, d = x2d.shape
    grid = (n // tm,)
    row = lambda i: (i, 0)
    const = lambda i: (0, 0)
    out_shape = (
        jax.ShapeDtypeStruct((n, ATTN_WIDTH), BF16),
        jax.ShapeDtypeStruct((n, ATTN_WIDTH), BF16),
        jax.ShapeDtypeStruct((n, ATTN_WIDTH), BF16),
        jax.ShapeDtypeStruct((n, ATTN_WIDTH), BF16),
        jax.ShapeDtypeStruct((n, LANES), BF16),
        jax.ShapeDtypeStruct((n, LANES), F32),
        jax.ShapeDtypeStruct((n, wu.shape[1]), F32),
        jax.ShapeDtypeStruct((n, wg.shape[1]), F32),
    )
    return pl.pallas_call(
        functools.partial(_inproj_kernel, q_scale=HEAD_DIM ** -0.5 * LOG2E),
        grid=grid,
        in_specs=[
            pl.BlockSpec((tm, d), row),
            pl.BlockSpec((1, d), const),
            pl.BlockSpec(wa.shape, const),
            pl.BlockSpec(wkw.shape, const),
            pl.BlockSpec(wu.shape, const),
            pl.BlockSpec(wg.shape, const),
        ],
        out_specs=[pl.BlockSpec((tm, s.shape[1]), row) for s in out_shape],
        out_shape=out_shape,
        compiler_params=pltpu.CompilerParams(
            dimension_semantics=("parallel",), vmem_limit_bytes=VMEM_LIMIT_BYTES),
        name="inproj",
    )(x2d, g, wa, wkw, wu, wg)


def _bias_kernel(tab_ref, out_ref, *, tile):
    r = lax.broadcasted_iota(I32, (tile, tile), 0)
    c = lax.broadcasted_iota(I32, (tile, tile), 1)
    for w in range(2):
        dist = c - r + w * tile
        for h in range(N_HEADS):
            far = tab_ref[N_BUCKETS - 1, h]
            val = jnp.zeros((tile, tile), F32)
            for b, (lo, hi) in enumerate(_BUCKET_RANGES):
                hit = (dist == lo) if lo == hi else ((dist >= lo) & (dist <= hi))
                val = jnp.where(hit, (tab_ref[b, h] - far) * LOG2E, val)
            out_ref[w, h] = val


def _bias_tiles(rel_bias, tile):
    return pl.pallas_call(
        functools.partial(_bias_kernel, tile=tile),
        in_specs=[pl.BlockSpec(memory_space=pltpu.SMEM)],
        out_specs=pl.BlockSpec(memory_space=pltpu.VMEM),
        out_shape=jax.ShapeDtypeStruct((2, N_HEADS, tile, tile), F32),
        name="bias_tiles",
    )(rel_bias)


def _attn_kernel(q_ref, qi_ref, wi_ref, k_ref, vt_ref, ki_ref, bias_ref, o_ref,
                 key_scr, acc_scr, m_scr, l_scr, jcut_scr, qpad_scr, qipad_scr,
                 *, tile, topk, seq_bits, wi_scale):
    t = tile
    i = pl.program_id(1)
    n_chunks = i + 1
    nt = (((1,), (1,)), ((), ()))

    lane = lax.broadcasted_iota(I32, (t, LANES), 1)
    for h in range(N_HEADS):
        pair = slice(LANES * (h // 2), LANES * (h // 2) + LANES)
        lo = HEAD_DIM * (h % 2)
        in_head = (lane >= lo) & (lane < lo + HEAD_DIM)
        qpad_scr[h] = jnp.where(in_head, q_ref[0, :, pair], 0).astype(BF16)
        qipad_scr[h] = jnp.where(in_head, qi_ref[0, :, pair], 0).astype(BF16)

    w_all = wi_ref[0] * wi_scale
    row = lax.broadcasted_iota(I32, (t, t), 0)
    col = lax.broadcasted_iota(I32, (t, t), 1)
    q_pos = i * t + col

    def score_chunk(j, carry):
        kij = ki_ref[0, pl.ds(pl.multiple_of(j * t, t), t), :]
        score = jnp.zeros((t, t), F32)
        for h in range(N_IDX_HEADS):
            logit = lax.dot_general(kij, qipad_scr[h], nt, preferred_element_type=F32)
            score = score + jnp.maximum(logit, 0.0) * w_all[h:h + 1, :]
        score = jnp.where(score == 0.0, 0.0, score)
        bits = pltpu.bitcast(score, I32)
        skey = bits ^ ((bits >> 31) & 0x7FFFFFFF)
        valid = (j * t + row) <= q_pos
        key_scr[j] = jnp.where(valid, skey, INT_MIN)
        return carry

    lax.fori_loop(0, n_chunks, score_chunk, 0)

    def count_ge(cand):
        def body(j, part):
            hit = jnp.where(key_scr[j] >= cand, 1, 0)
            return part + jnp.sum(hit.reshape(t // SUBLANES, SUBLANES, t), axis=0)
        part = lax.fori_loop(0, n_chunks, body, jnp.zeros((SUBLANES, t), I32))
        return jnp.sum(part, axis=0, keepdims=True)

    def bit_step(it, cur_u):
        cand_u = cur_u | (jnp.int32(1) << (31 - it))
        cnt = count_ge(cand_u ^ INT_MIN)
        return jnp.where(cnt >= topk, cand_u, cur_u)

    cur_u = lax.fori_loop(0, 32, bit_step, jnp.zeros((1, t), I32))
    thr = cur_u ^ INT_MIN

    def count_gt_ge(j, carry):
        gt, ge = carry
        kc = key_scr[j]
        gt = gt + jnp.sum(jnp.where(kc > thr, 1, 0).reshape(t // SUBLANES, SUBLANES, t), axis=0)
        ge = ge + jnp.sum(jnp.where(kc >= thr, 1, 0).reshape(t // SUBLANES, SUBLANES, t), axis=0)
        return gt, ge

    zero8 = jnp.zeros((SUBLANES, t), I32)
    gt8, ge8 = lax.fori_loop(0, n_chunks, count_gt_ge, (zero8, zero8))
    c_gt = jnp.sum(gt8, axis=0, keepdims=True)
    c_ge = jnp.sum(ge8, axis=0, keepdims=True)
    need = topk - c_gt
    has_thr = thr > INT_MIN
    tie_split = (c_ge > topk) & has_thr
    jcut_scr[0:1, :] = jnp.where(has_thr, jnp.int32(2 ** 30), jnp.int32(-1))

    @pl.when(jnp.max(jnp.where(tie_split, 1, 0)) > 0)
    def _():
        def tie_bit(it, cur):
            cand = cur | (jnp.int32(1) << (seq_bits - 1 - it))

            def body(j, part):
                idx = j * t + row
                hit = jnp.where(key_scr[j] == thr, jnp.where(idx < cand, 1, 0), 0)
                return part + jnp.sum(hit.reshape(t // SUBLANES, SUBLANES, t), axis=0)

            part = lax.fori_loop(0, n_chunks, body, jnp.zeros((SUBLANES, t), I32))
            cnt = jnp.sum(part, axis=0, keepdims=True)
            return jnp.where(cnt < need, cand, cur)

        p = lax.fori_loop(0, seq_bits, tie_bit, jnp.zeros((1, t), I32))
        jcut_scr[0:1, :] = jnp.where(tie_split, p, jcut_scr[0:1, :])

    jcut = jcut_scr[0:1, :]

    def mask_chunk(j, carry):
        kc = key_scr[j]
        idx = j * t + row
        tie_ok = jnp.where(idx <= jcut, 0.0, NEG_BIG)
        mb = jnp.where(kc > thr, 0.0, jnp.where(kc == thr, tie_ok, NEG_BIG))
        key_scr[j] = pltpu.bitcast(mb.astype(F32), I32)
        return carry

    lax.fori_loop(0, n_chunks, mask_chunk, 0)

    m_scr[...] = jnp.full(m_scr.shape, -jnp.inf, F32)
    l_scr[...] = jnp.zeros(l_scr.shape, F32)
    acc_scr[...] = jnp.zeros(acc_scr.shape, F32)

    def attend(j, bias_sel):
        mb = pltpu.bitcast(key_scr[j], F32)
        for h in range(N_HEADS):
            pair = slice(LANES * (h // 2), LANES * (h // 2) + LANES)
            rows = slice(HEAD_DIM * h, HEAD_DIM * (h + 1))
            kc = k_ref[0, pl.ds(pl.multiple_of(j * t, t), t), pair]
            s = lax.dot_general(kc, qpad_scr[h], nt, preferred_element_type=F32) + mb
            if bias_sel is not None:
                s = s + bias_ref[bias_sel, h]
            m_old = m_scr[h:h + 1, :]
            m_new = jnp.maximum(m_old, jnp.max(s, axis=0, keepdims=True))
            alpha = jnp.exp2(m_old - m_new)
            p = jnp.exp2(s - m_new)
            l_scr[h:h + 1, :] = alpha * l_scr[h:h + 1, :] + jnp.sum(p, axis=0, keepdims=True)
            pv = jnp.dot(vt_ref[0, j, rows, :], p.astype(BF16), preferred_element_type=F32)
            acc_scr[rows, :] = alpha * acc_scr[rows, :] + pv
            m_scr[h:h + 1, :] = m_new

    def far_chunk(j, carry):
        attend(j, None)
        return carry

    lax.fori_loop(0, jnp.maximum(i - 1, 0), far_chunk, 0)

    @pl.when(i >= 1)
    def _():
        attend(i - 1, 1)

    attend(i, 0)

    inv_l = 1.0 / l_scr[...]
    for h in range(N_HEADS):
        rows = slice(HEAD_DIM * h, HEAD_DIM * (h + 1))
        acc_scr[rows, :] = acc_scr[rows, :] * inv_l[h:h + 1, :]
    o_ref[0] = acc_scr[...].T.astype(BF16)


def _attention(q, qi, wit, k, vt4, ki2, bias, *, tile, topk):
    b, s, _ = q.shape
    nq = s // tile
    assert tile >= topk and tile >= _BIAS_REACH and s % tile == 0 and (s & (s - 1)) == 0
    seq_bits = s.bit_length() - 1
    kern = functools.partial(_attn_kernel, tile=tile, topk=topk, seq_bits=seq_bits,
                             wi_scale=(N_IDX_HEADS ** -0.5) * (IDX_DIM ** -0.5))
    return pl.pallas_call(
        kern,
        grid=(b, nq),
        in_specs=[
            pl.BlockSpec((1, tile, ATTN_WIDTH), lambda bb, i: (bb, i, 0)),
            pl.BlockSpec((1, tile, ATTN_WIDTH), lambda bb, i: (bb, i, 0)),
            pl.BlockSpec((1, N_IDX_HEADS, tile), lambda bb, i: (bb, 0, i)),
            pl.BlockSpec((1, s, ATTN_WIDTH), lambda bb, i: (bb, 0, 0)),
            pl.BlockSpec((1, nq, ATTN_WIDTH, tile), lambda bb, i: (bb, 0, 0, 0)),
            pl.BlockSpec((1, s, LANES), lambda bb, i: (bb, 0, 0)),
            pl.BlockSpec((2, N_HEADS, tile, tile), lambda bb, i: (0, 0, 0, 0)),
        ],
        out_specs=pl.BlockSpec((1, tile, ATTN_WIDTH), lambda bb, i: (bb, i, 0)),
        out_shape=jax.ShapeDtypeStruct((b, s, ATTN_WIDTH), BF16),
        scratch_shapes=[
            pltpu.VMEM((nq, tile, tile), I32),
            pltpu.VMEM((ATTN_WIDTH, tile), F32),
            pltpu.VMEM((N_HEADS, tile), F32),
            pltpu.VMEM((N_HEADS, tile), F32),
            pltpu.VMEM((SUBLANES, tile), I32),
            pltpu.VMEM((N_HEADS, tile, LANES), BF16),
            pltpu.VMEM((N_IDX_HEADS, tile, LANES), BF16),
        ],
        compiler_params=pltpu.CompilerParams(
            dimension_semantics=("arbitrary", "arbitrary"), vmem_limit_bytes=VMEM_LIMIT_BYTES),
        name="dsa_attention",
    )(q, qi, wit, k, vt4, ki2, bias)


def _conv_kernel(u_ref, dw_ref, db_ref, g_ref, b_ref, o_ref, h_scr, *, tc, sub):
    ch = o_ref.shape[-1]
    halo = CONV_HALO

    @pl.when(pl.program_id(1) == 0)
    def _():
        h_scr[0:halo, :] = jnp.zeros((halo, ch), F32)

    u = u_ref[0]
    h_scr[halo:halo + tc, :] = u[:, :ch] * _sigmoid(u[:, ch:])
    first = halo - (CONV_WIDTH - 1)

    for r in range(tc // sub):
        base = r * sub
        acc = jnp.zeros((sub, ch), F32)
        for j in range(CONV_WIDTH):
            acc = acc + dw_ref[j:j + 1, :] * h_scr[base + first + j:base + first + j + sub, :]
        acc = acc + db_ref[...]
        mu = jnp.mean(acc, axis=-1, keepdims=True)
        cen = acc - mu
        var = jnp.mean(cen * cen, axis=-1, keepdims=True)
        y = cen * lax.rsqrt(var + EPS) * g_ref[...] + b_ref[...]
        o_ref[0, base:base + sub, :] = (y * _sigmoid(y)).astype(o_ref.dtype)
    h_scr[0:halo, :] = h_scr[tc:tc + halo, :]


def _conformer_conv(u, dw, db, g, bb, *, tc, sub=64):
    b, s, c2 = u.shape
    ch = c2 // 2
    const = lambda bi, i: (0, 0)
    return pl.pallas_call(
        functools.partial(_conv_kernel, tc=tc, sub=sub),
        grid=(b, s // tc),
        in_specs=[
            pl.BlockSpec((1, tc, c2), lambda bi, i: (bi, i, 0)),
            pl.BlockSpec(dw.shape, const),
            pl.BlockSpec((1, ch), const),
            pl.BlockSpec((1, ch), const),
            pl.BlockSpec((1, ch), const),
        ],
        out_specs=pl.BlockSpec((1, tc, ch), lambda bi, i: (bi, i, 0)),
        out_shape=jax.ShapeDtypeStruct((b, s, ch), BF16),
        scratch_shapes=[pltpu.VMEM((tc + CONV_HALO, ch), F32)],
        compiler_params=pltpu.CompilerParams(
            dimension_semantics=("arbitrary", "arbitrary"), vmem_limit_bytes=VMEM_LIMIT_BYTES),
        name="conformer_conv",
    )(u, dw, db, g, bb)


def _mix_kernel(x_ref, a_ref, c_ref, gg_ref, wa_ref, wc_ref, wm_ref, o_ref):
    d = x_ref.shape[-1]
    y_a = jnp.dot(a_ref[...], wa_ref[...], preferred_element_type=F32)
    y_b = jnp.dot(c_ref[...], wc_ref[...], preferred_element_type=F32)
    gg = gg_ref[...]
    merged = _sigmoid(gg[:, :d]) * y_a + _sigmoid(gg[:, d:]) * y_b
    o_ref[...] = x_ref[...] + jnp.dot(merged.astype(BF16), wm_ref[...], preferred_element_type=F32)


def _mix(x2d, attn, conv, gg, wa, wc, wm, *, tm):
    n, d = x2d.shape
    row = lambda i: (i, 0)
    const = lambda i: (0, 0)
    return pl.pallas_call(
        _mix_kernel,
        grid=(n // tm,),
        in_specs=[
            pl.BlockSpec((tm, d), row),
            pl.BlockSpec((tm, attn.shape[1]), row),
            pl.BlockSpec((tm, conv.shape[1]), row),
            pl.BlockSpec((tm, gg.shape[1]), row),
            pl.BlockSpec(wa.shape, const),
            pl.BlockSpec(wc.shape, const),
            pl.BlockSpec(wm.shape, const),
        ],
        out_specs=pl.BlockSpec((tm, d), row),
        out_shape=jax.ShapeDtypeStruct((n, d), F32),
        compiler_params=pltpu.CompilerParams(
            dimension_semantics=("parallel",), vmem_limit_bytes=VMEM_LIMIT_BYTES),
        name="gated_mix",
    )(x2d, attn, conv, gg, wa, wc, wm)


def _ffn_kernel(x_ref, g_ref, wg_ref, wu_ref, wo_ref, fg_ref, o_ref, h_scr, acc_scr, *, final_norm):
    j = pl.program_id(1)

    @pl.when(j == 0)
    def _():
        x = x_ref[...]
        ms = jnp.mean(x * x, axis=-1, keepdims=True)
        h_scr[...] = ((x * lax.rsqrt(ms + EPS)) * g_ref[...]).astype(BF16)
        acc_scr[...] = jnp.zeros(acc_scr.shape, F32)

    h = h_scr[...]
    gate = jnp.dot(h, wg_ref[...], preferred_element_type=F32)
    up = jnp.dot(h, wu_ref[...], preferred_element_type=F32)
    act = (gate * _sigmoid(gate) * up).astype(BF16)
    acc_scr[...] += jnp.dot(act, wo_ref[...], preferred_element_type=F32)

    @pl.when(j == pl.num_programs(1) - 1)
    def _():
        y = x_ref[...] + acc_scr[...]
        if final_norm:
            ms = jnp.mean(y * y, axis=-1, keepdims=True)
            y = (y * lax.rsqrt(ms + EPS)) * fg_ref[...]
        o_ref[...] = y


def _ffn(x2d, g, w_gate, w_up, w_out, fg, *, tm, tf, final_norm):
    n, d = x2d.shape
    dff = w_gate.shape[1]
    return pl.pallas_call(
        functools.partial(_ffn_kernel, final_norm=final_norm),
        grid=(n // tm, dff // tf),
        in_specs=[
            pl.BlockSpec((tm, d), lambda i, j: (i, 0)),
            pl.BlockSpec((1, d), lambda i, j: (0, 0)),
            pl.BlockSpec((d, tf), lambda i, j: (0, j)),
            pl.BlockSpec((d, tf), lambda i, j: (0, j)),
            pl.BlockSpec((tf, d), lambda i, j: (j, 0)),
            pl.BlockSpec((1, d), lambda i, j: (0, 0)),
        ],
        out_specs=pl.BlockSpec((tm, d), lambda i, j: (i, 0)),
        out_shape=jax.ShapeDtypeStruct((n, d), F32),
        scratch_shapes=[pltpu.VMEM((tm, d), BF16), pltpu.VMEM((tm, d), F32)],
        compiler_params=pltpu.CompilerParams(
            dimension_semantics=("parallel", "arbitrary"), vmem_limit_bytes=VMEM_LIMIT_BYTES),
        name="swiglu_ffn",
    )(x2d, g, w_gate, w_up, w_out, fg)


def _pick_tile(n, want):
    t = min(n, want)
    while n % t:
        t //= 2
    return t


def kernel(x, rel_bias, mix_norm, w_in, w_attn_out, dw_kernel, dw_bias, conv_norm_g, conv_norm_b,
           w_conv_out, w_mix_out, ffn_norm, w_ffn_in, w_ffn_out, final_norm):
    b, s, d = x.shape
    depth = w_in.shape[0]
    n = b * s
    conv_ch = d // 2
    d_ff = w_ffn_out.shape[1]
    topk = min(TOPK_MAX, s // 4)
    tile = ATTN_TILE
    nq = s // tile

    sizes = (ATTN_WIDTH, ATTN_WIDTH, ATTN_WIDTH, N_IDX_HEADS * IDX_DIM, IDX_DIM, N_IDX_HEADS,
             2 * conv_ch, d, d)
    offs = np.concatenate([[0], np.cumsum(sizes)])
    o_ki, o_wi, o_u, o_g = offs[4], offs[5], offs[6], offs[7]

    bias = _bias_tiles(rel_bias.astype(F32), tile)

    x2d = x.reshape(n, d).astype(F32)
    for l in range(depth):
        w = w_in[l]
        wa = w[:, :o_ki].astype(BF16)
        w_ki = w[:, o_ki:o_wi]
        w_wi = w[:, o_wi:o_u]
        wkw = jnp.concatenate(
            [w_ki, w_ki, w_wi, jnp.zeros((d, LANES - N_IDX_HEADS), w.dtype)], axis=1).astype(BF16)
        wu = w[:, o_u:o_g].astype(BF16)
        wg = w[:, o_g:].astype(BF16)

        q, k, v, qi, ki2, wi, u, gg = _inproj(
            x2d, mix_norm[l].reshape(1, d).astype(F32), wa, wkw, wu, wg, tm=_pick_tile(n, 512))

        vt4 = v.reshape(b, nq, tile, ATTN_WIDTH).transpose(0, 1, 3, 2)
        wit = wi[:, :N_IDX_HEADS].reshape(b, s, N_IDX_HEADS).transpose(0, 2, 1)
        attn = _attention(q.reshape(b, s, -1), qi.reshape(b, s, -1), wit, k.reshape(b, s, -1), vt4,
                          ki2.reshape(b, s, -1), bias, tile=tile, topk=topk)

        conv = _conformer_conv(
            u.reshape(b, s, -1), dw_kernel[l].astype(F32), dw_bias[l].reshape(1, -1).astype(F32),
            conv_norm_g[l].reshape(1, -1).astype(F32), conv_norm_b[l].reshape(1, -1).astype(F32),
            tc=_pick_tile(s, 512))

        x2d = _mix(x2d, attn.reshape(n, -1), conv.reshape(n, -1), gg,
                   w_attn_out[l].astype(BF16), w_conv_out[l].astype(BF16), w_mix_out[l].astype(BF16),
                   tm=_pick_tile(n, 512))

        wf = w_ffn_in[l]
        x2d = _ffn(x2d, ffn_norm[l].reshape(1, d).astype(F32),
                   wf[:, :d_ff].astype(BF16), wf[:, d_ff:].astype(BF16), w_ffn_out[l].astype(BF16),
                   final_norm.reshape(1, d).astype(F32),
                   tm=_pick_tile(n, 1024), tf=256, final_norm=(l == depth - 1))
    return x2d.reshape(b, s, d).astype(x.dtype)
```

```python
import functools
import math

import numpy as np
import jax
import jax.numpy as jnp
from jax import lax
from jax.experimental import pallas as pl
from jax.experimental.pallas import tpu as pltpu

F32 = jnp.float32
BF16 = jnp.bfloat16
I32 = jnp.int32

N_HEADS = 8
HEAD_DIM = 64
ATTN_WIDTH = N_HEADS * HEAD_DIM
N_IDX_HEADS = 8
IDX_DIM = 64
TOPK_MAX = 256
CONV_WIDTH = 31
N_BUCKETS = 32
MAX_DISTANCE = 128
EPS = 1e-6

LANES = 128
SUBLANES = 8
VMEM_LIMIT_BYTES = 56 * 1024 * 1024
LOG2E = math.log2(math.e)
NEG_BIG = -1e30
INT_MIN = -(2 ** 31)

ATTN_TILE = 256
CONV_HALO = 32


def _sigmoid(x):
    return 1.0 / (1.0 + jnp.exp(-x))


def _t5_bucket_ranges():
    max_exact = N_BUCKETS // 2
    buckets = []
    for n in range(0, 4 * MAX_DISTANCE):
        if n < max_exact:
            buckets.append(n)
        else:
            b = max_exact + int(math.log(n / max_exact) / math.log(MAX_DISTANCE / max_exact)
                                * (N_BUCKETS - max_exact))
            buckets.append(min(b, N_BUCKETS - 1))
    ranges = []
    for b in range(N_BUCKETS - 1):
        ns = [n for n, bb in enumerate(buckets) if bb == b]
        ranges.append((min(ns), max(ns)))
    return ranges


_BUCKET_RANGES = _t5_bucket_ranges()
_BIAS_REACH = _BUCKET_RANGES[-1][1] + 1


def _inproj_kernel(x_ref, g_ref, wa_ref, wkw_ref, wu_ref, wg_ref,
                   q_ref, k_ref, v_ref, qi_ref, ki_ref, wi_ref, u_ref, gg_ref, *, q_scale):
    x = x_ref[...]
    ms = jnp.mean(x * x, axis=-1, keepdims=True)
    h = ((x * lax.rsqrt(ms + EPS)) * g_ref[...]).astype(BF16)

    def mm(w):
        return jnp.dot(h, w, preferred_element_type=F32)

    w = ATTN_WIDTH
    q_ref[...] = (mm(wa_ref[:, 0:w]) * q_scale).astype(BF16)
    k_ref[...] = mm(wa_ref[:, w:2 * w]).astype(BF16)
    v_ref[...] = mm(wa_ref[:, 2 * w:3 * w]).astype(BF16)
    qi_ref[...] = mm(wa_ref[:, 3 * w:4 * w]).astype(BF16)
    kw = mm(wkw_ref[...])
    ki_ref[...] = kw[:, :LANES].astype(BF16)
    wi_ref[...] = kw[:, LANES:]
    u_ref[...] = mm(wu_ref[...])
    gg_ref[...] = mm(wg_ref[...])


def _inproj(x2d, g, wa, wkw, wu, wg, *, tm):
    n, d = x2d.shape
    grid = (n // tm,)
    row = lambda i: (i, 0)
    const = lambda i: (0, 0)
    out_shape = (
        jax.ShapeDtypeStruct((n, ATTN_WIDTH), BF16),
        jax.ShapeDtypeStruct((n, ATTN_WIDTH), BF16),
        jax.ShapeDtypeStruct((n, ATTN_WIDTH), BF16),
        jax.ShapeDtypeStruct((n, ATTN_WIDTH), BF16),
        jax.ShapeDtypeStruct((n, LANES), BF16),
        jax.ShapeDtypeStruct((n, LANES), F32),
        jax.ShapeDtypeStruct((n, wu.shape[1]), F32),
        jax.ShapeDtypeStruct((n, wg.shape[1]), F32),
    )
    return pl.pallas_call(
        functools.partial(_inproj_kernel, q_scale=HEAD_DIM ** -0.5 * LOG2E),
        grid=grid,
        in_specs=[
            pl.BlockSpec((tm, d), row),
            pl.BlockSpec((1, d), const),
            pl.BlockSpec(wa.shape, const),
            pl.BlockSpec(wkw.shape, const),
            pl.BlockSpec(wu.shape, const),
            pl.BlockSpec(wg.shape, const),
        ],
        out_specs=[pl.BlockSpec((tm, s.shape[1]), row) for s in out_shape],
        out_shape=out_shape,
        compiler_params=pltpu.CompilerParams(
            dimension_semantics=("parallel",), vmem_limit_bytes=VMEM_LIMIT_BYTES),
        name="inproj",
    )(x2d, g, wa, wkw, wu, wg)


def _bias_kernel(tab_ref, out_ref, *, tile):
    r = lax.broadcasted_iota(I32, (tile, tile), 0)
    c = lax.broadcasted_iota(I32, (tile, tile), 1)
    for w in range(2):
        dist = c - r + w * tile
        for h in range(N_HEADS):
            far = tab_ref[N_BUCKETS - 1, h]
            val = jnp.zeros((tile, tile), F32)
            for b, (lo, hi) in enumerate(_BUCKET_RANGES):
                hit = (dist == lo) if lo == hi else ((dist >= lo) & (dist <= hi))
                val = jnp.where(hit, (tab_ref[b, h] - far) * LOG2E, val)
            out_ref[w, h] = val


def _bias_tiles(rel_bias, tile):
    return pl.pallas_call(
        functools.partial(_bias_kernel, tile=tile),
        in_specs=[pl.BlockSpec(memory_space=pltpu.SMEM)],
        out_specs=pl.BlockSpec(memory_space=pltpu.VMEM),
        out_shape=jax.ShapeDtypeStruct((2, N_HEADS, tile, tile), F32),
        name="bias_tiles",
    )(rel_bias)


def _attn_kernel(qt_ref, qit_ref, wi_ref, k_ref, vt_ref, ki_ref, bias_ref, o_ref,
                 key_scr, s_scr, acc_scr, m_scr, l_scr, jcut_scr, qpad_scr, qipad_scr,
                 *, tile, topk, seq_bits, wi_scale):
    t = tile
    i = pl.program_id(1)
    n_chunks = i + 1

    feat = lax.broadcasted_iota(I32, (LANES, t), 0)
    for h in range(N_HEADS):
        pair = slice(LANES * (h // 2), LANES * (h // 2) + LANES)
        lo = HEAD_DIM * (h % 2)
        in_head = (feat >= lo) & (feat < lo + HEAD_DIM)
        qpad_scr[h] = jnp.where(in_head, qt_ref[0, pair, :], 0).astype(BF16)
        qipad_scr[h] = jnp.where(in_head, qit_ref[0, pair, :], 0).astype(BF16)

    w_all = wi_ref[0] * wi_scale
    row = lax.broadcasted_iota(I32, (t, t), 0)
    col = lax.broadcasted_iota(I32, (t, t), 1)
    q_pos = i * t + col

    def score_chunk(j, carry):
        kij = ki_ref[0, pl.ds(pl.multiple_of(j * t, t), t), :]
        for hp in range(N_IDX_HEADS // 2):
            ha, hb = 2 * hp, 2 * hp + 1
            la = jnp.dot(kij, qipad_scr[ha], preferred_element_type=F32)
            lb = jnp.dot(kij, qipad_scr[hb], preferred_element_type=F32)
            term = (jnp.maximum(la, 0.0) * w_all[ha:ha + 1, :]
                    + jnp.maximum(lb, 0.0) * w_all[hb:hb + 1, :])
            if hp == 0:
                s_scr[0] = term
            elif hp < N_IDX_HEADS // 2 - 1:
                s_scr[0] = s_scr[0] + term
            else:
                score = s_scr[0] + term
                score = jnp.where(score == 0.0, 0.0, score)
                bits = pltpu.bitcast(score, I32)
                skey = bits ^ ((bits >> 31) & 0x7FFFFFFF)
                valid = (j * t + row) <= q_pos
                key_scr[j] = jnp.where(valid, skey, INT_MIN)
        return carry

    lax.fori_loop(0, n_chunks, score_chunk, 0)

    def count_ge(cand):
        def body(j, part):
            hit = jnp.where(key_scr[j] >= cand, 1, 0)
            return part + jnp.sum(hit.reshape(t // SUBLANES, SUBLANES, t), axis=0)
        part = lax.fori_loop(0, n_chunks, body, jnp.zeros((SUBLANES, t), I32))
        return jnp.sum(part, axis=0, keepdims=True)

    def bit_step(it, cur_u):
        cand_u = cur_u | (jnp.int32(1) << (31 - it))
        cnt = count_ge(cand_u ^ INT_MIN)
        return jnp.where(cnt >= topk, cand_u, cur_u)

    cur_u = lax.fori_loop(0, 32, bit_step, jnp.zeros((1, t), I32))
    thr = cur_u ^ INT_MIN

    def count_gt_ge(j, carry):
        gt, ge = carry
        kc = key_scr[j]
        gt = gt + jnp.sum(jnp.where(kc > thr, 1, 0).reshape(t // SUBLANES, SUBLANES, t), axis=0)
        ge = ge + jnp.sum(jnp.where(kc >= thr, 1, 0).reshape(t // SUBLANES, SUBLANES, t), axis=0)
        return gt, ge

    zero8 = jnp.zeros((SUBLANES, t), I32)
    gt8, ge8 = lax.fori_loop(0, n_chunks, count_gt_ge, (zero8, zero8))
    c_gt = jnp.sum(gt8, axis=0, keepdims=True)
    c_ge = jnp.sum(ge8, axis=0, keepdims=True)
    need = topk - c_gt
    has_thr = thr > INT_MIN
    tie_split = (c_ge > topk) & has_thr
    jcut_scr[0:1, :] = jnp.where(has_thr, jnp.int32(2 ** 30), jnp.int32(-1))

    @pl.when(jnp.max(jnp.where(tie_split, 1, 0)) > 0)
    def _():
        def tie_bit(it, cur):
            cand = cur | (jnp.int32(1) << (seq_bits - 1 - it))

            def body(j, part):
                idx = j * t + row
                hit = jnp.where(key_scr[j] == thr, jnp.where(idx < cand, 1, 0), 0)
                return part + jnp.sum(hit.reshape(t // SUBLANES, SUBLANES, t), axis=0)

            part = lax.fori_loop(0, n_chunks, body, jnp.zeros((SUBLANES, t), I32))
            cnt = jnp.sum(part, axis=0, keepdims=True)
            return jnp.where(cnt < need, cand, cur)

        p = lax.fori_loop(0, seq_bits, tie_bit, jnp.zeros((1, t), I32))
        jcut_scr[0:1, :] = jnp.where(tie_split, p, jcut_scr[0:1, :])

    jcut = jcut_scr[0:1, :]

    def mask_chunk(j, carry):
        kc = key_scr[j]
        idx = j * t + row
        tie_ok = jnp.where(idx <= jcut, 0.0, NEG_BIG)
        mb = jnp.where(kc > thr, 0.0, jnp.where(kc == thr, tie_ok, NEG_BIG))
        key_scr[j] = pltpu.bitcast(mb.astype(F32), I32)
        return carry

    lax.fori_loop(0, n_chunks, mask_chunk, 0)

    m_scr[...] = jnp.full(m_scr.shape, -jnp.inf, F32)
    l_scr[...] = jnp.zeros(l_scr.shape, F32)
    acc_scr[...] = jnp.zeros(acc_scr.shape, F32)

    def attend(j, bias_sel):
        alphas = []
        for h in range(N_HEADS):
            pair = slice(LANES * (h // 2), LANES * (h // 2) + LANES)
            kc = k_ref[0, pl.ds(pl.multiple_of(j * t, t), t), pair]
            s = jnp.dot(kc, qpad_scr[h], preferred_element_type=F32) + pltpu.bitcast(key_scr[j], F32)
            if bias_sel is not None:
                s = s + bias_ref[bias_sel, h]
            s_scr[h] = s
            m_old = m_scr[h:h + 1, :]
            m_new = jnp.maximum(m_old, jnp.max(s, axis=0, keepdims=True))
            alphas.append(jnp.exp2(m_old - m_new))
            m_scr[h:h + 1, :] = m_new
        for h in range(N_HEADS):
            rows = slice(HEAD_DIM * h, HEAD_DIM * (h + 1))
            p = jnp.exp2(s_scr[h] - m_scr[h:h + 1, :])
            l_scr[h:h + 1, :] = alphas[h] * l_scr[h:h + 1, :] + jnp.sum(p, axis=0, keepdims=True)
            pv = jnp.dot(vt_ref[0, j, rows, :], p.astype(BF16), preferred_element_type=F32)
            acc_scr[rows, :] = alphas[h] * acc_scr[rows, :] + pv

    def far_chunk(j, carry):
        attend(j, None)
        return carry

    lax.fori_loop(0, jnp.maximum(i - 1, 0), far_chunk, 0)

    @pl.when(i >= 1)
    def _():
        attend(i - 1, 1)

    attend(i, 0)

    inv_l = 1.0 / l_scr[...]
    for h in range(N_HEADS):
        rows = slice(HEAD_DIM * h, HEAD_DIM * (h + 1))
        acc_scr[rows, :] = acc_scr[rows, :] * inv_l[h:h + 1, :]
    o_ref[0] = acc_scr[...].T.astype(BF16)


def _attention(qt, qit, wit, k, vt4, ki2, bias, *, tile, topk):
    b, s, _ = k.shape
    nq = s // tile
    assert tile >= topk and tile >= _BIAS_REACH and s % tile == 0 and (s & (s - 1)) == 0
    seq_bits = s.bit_length() - 1
    kern = functools.partial(_attn_kernel, tile=tile, topk=topk, seq_bits=seq_bits,
                             wi_scale=(N_IDX_HEADS ** -0.5) * (IDX_DIM ** -0.5))
    resident = pl.Buffered(1)
    return pl.pallas_call(
        kern,
        grid=(b, nq),
        in_specs=[
            pl.BlockSpec((1, ATTN_WIDTH, tile), lambda bb, i: (bb, 0, i)),
            pl.BlockSpec((1, ATTN_WIDTH, tile), lambda bb, i: (bb, 0, i)),
            pl.BlockSpec((1, N_IDX_HEADS, tile), lambda bb, i: (bb, 0, i)),
            pl.BlockSpec((1, s, ATTN_WIDTH), lambda bb, i: (bb, 0, 0), pipeline_mode=resident),
            pl.BlockSpec((1, nq, ATTN_WIDTH, tile), lambda bb, i: (bb, 0, 0, 0),
                         pipeline_mode=resident),
            pl.BlockSpec((1, s, LANES), lambda bb, i: (bb, 0, 0), pipeline_mode=resident),
            pl.BlockSpec((2, N_HEADS, tile, tile), lambda bb, i: (0, 0, 0, 0),
                         pipeline_mode=resident),
        ],
        out_specs=pl.BlockSpec((1, tile, ATTN_WIDTH), lambda bb, i: (bb, i, 0)),
        out_shape=jax.ShapeDtypeStruct((b, s, ATTN_WIDTH), BF16),
        scratch_shapes=[
            pltpu.VMEM((nq, tile, tile), I32),
            pltpu.VMEM((N_HEADS, tile, tile), F32),
            pltpu.VMEM((ATTN_WIDTH, tile), F32),
            pltpu.VMEM((N_HEADS, tile), F32),
            pltpu.VMEM((N_HEADS, tile), F32),
            pltpu.VMEM((SUBLANES, tile), I32),
            pltpu.VMEM((N_HEADS, LANES, tile), BF16),
            pltpu.VMEM((N_IDX_HEADS, LANES, tile), BF16),
        ],
        compiler_params=pltpu.CompilerParams(
            dimension_semantics=("arbitrary", "arbitrary"), vmem_limit_bytes=VMEM_LIMIT_BYTES),
        name="dsa_attention",
    )(qt, qit, wit, k, vt4, ki2, bias)


def _conv_kernel(u_ref, dw_ref, db_ref, g_ref, b_ref, o_ref, h_scr, *, tc, sub):
    ch = o_ref.shape[-1]
    halo = CONV_HALO

    @pl.when(pl.program_id(1) == 0)
    def _():
        h_scr[0:halo, :] = jnp.zeros((halo, ch), F32)

    u = u_ref[0]
    h_scr[halo:halo + tc, :] = u[:, :ch] * _sigmoid(u[:, ch:])
    first = halo - (CONV_WIDTH - 1)

    for r in range(tc // sub):
        base = r * sub
        acc = jnp.zeros((sub, ch), F32)
        for j in range(CONV_WIDTH):
            acc = acc + dw_ref[j:j + 1, :] * h_scr[base + first + j:base + first + j + sub, :]
        acc = acc + db_ref[...]
        mu = jnp.mean(acc, axis=-1, keepdims=True)
        cen = acc - mu
        var = jnp.mean(cen * cen, axis=-1, keepdims=True)
        y = cen * lax.rsqrt(var + EPS) * g_ref[...] + b_ref[...]
        o_ref[0, base:base + sub, :] = (y * _sigmoid(y)).astype(o_ref.dtype)
    h_scr[0:halo, :] = h_scr[tc:tc + halo, :]


def _conformer_conv(u, dw, db, g, bb, *, tc, sub=64):
    b, s, c2 = u.shape
    ch = c2 // 2
    const = lambda bi, i: (0, 0)
    return pl.pallas_call(
        functools.partial(_conv_kernel, tc=tc, sub=sub),
        grid=(b, s // tc),
        in_specs=[
            pl.BlockSpec((1, tc, c2), lambda bi, i: (bi, i, 0)),
            pl.BlockSpec(dw.shape, const),
            pl.BlockSpec((1, ch), const),
            pl.BlockSpec((1, ch), const),
            pl.BlockSpec((1, ch), const),
        ],
        out_specs=pl.BlockSpec((1, tc, ch), lambda bi, i: (bi, i, 0)),
        out_shape=jax.ShapeDtypeStruct((b, s, ch), BF16),
        scratch_shapes=[pltpu.VMEM((tc + CONV_HALO, ch), F32)],
        compiler_params=pltpu.CompilerParams(
            dimension_semantics=("arbitrary", "arbitrary"), vmem_limit_bytes=VMEM_LIMIT_BYTES),
        name="conformer_conv",
    )(u, dw, db, g, bb)


def _mix_kernel(x_ref, a_ref, c_ref, gg_ref, wa_ref, wc_ref, wm_ref, o_ref):
    d = x_ref.shape[-1]
    y_a = jnp.dot(a_ref[...], wa_ref[...], preferred_element_type=F32)
    y_b = jnp.dot(c_ref[...], wc_ref[...], preferred_element_type=F32)
    gg = gg_ref[...]
    merged = _sigmoid(gg[:, :d]) * y_a + _sigmoid(gg[:, d:]) * y_b
    o_ref[...] = x_ref[...] + jnp.dot(merged.astype(BF16), wm_ref[...], preferred_element_type=F32)


def _mix(x2d, attn, conv, gg, wa, wc, wm, *, tm):
    n, d = x2d.shape
    row = lambda i: (i, 0)
    const = lambda i: (0, 0)
    return pl.pallas_call(
        _mix_kernel,
        grid=(n // tm,),
        in_specs=[
            pl.BlockSpec((tm, d), row),
            pl.BlockSpec((tm, attn.shape[1]), row),
            pl.BlockSpec((tm, conv.shape[1]), row),
            pl.BlockSpec((tm, gg.shape[1]), row),
            pl.BlockSpec(wa.shape, const),
            pl.BlockSpec(wc.shape, const),
            pl.BlockSpec(wm.shape, const),
        ],
        out_specs=pl.BlockSpec((tm, d), row),
        out_shape=jax.ShapeDtypeStruct((n, d), F32),
        compiler_params=pltpu.CompilerParams(
            dimension_semantics=("parallel",), vmem_limit_bytes=VMEM_LIMIT_BYTES),
        name="gated_mix",
    )(x2d, attn, conv, gg, wa, wc, wm)


def _ffn_kernel(x_ref, g_ref, wg_ref, wu_ref, wo_ref, fg_ref, o_ref, h_scr, acc_scr, *, final_norm):
    j = pl.program_id(1)

    @pl.when(j == 0)
    def _():
        x = x_ref[...]
        ms = jnp.mean(x * x, axis=-1, keepdims=True)
        h_scr[...] = ((x * lax.rsqrt(ms + EPS)) * g_ref[...]).astype(BF16)
        acc_scr[...] = jnp.zeros(acc_scr.shape, F32)

    h = h_scr[...]
    gate = jnp.dot(h, wg_ref[...], preferred_element_type=F32)
    up = jnp.dot(h, wu_ref[...], preferred_element_type=F32)
    act = (gate * _sigmoid(gate) * up).astype(BF16)
    acc_scr[...] += jnp.dot(act, wo_ref[...], preferred_element_type=F32)

    @pl.when(j == pl.num_programs(1) - 1)
    def _():
        y = x_ref[...] + acc_scr[...]
        if final_norm:
            ms = jnp.mean(y * y, axis=-1, keepdims=True)
            y = (y * lax.rsqrt(ms + EPS)) * fg_ref[...]
        o_ref[...] = y


def _ffn(x2d, g, w_gate, w_up, w_out, fg, *, tm, tf, final_norm):
    n, d = x2d.shape
    dff = w_gate.shape[1]
    return pl.pallas_call(
        functools.partial(_ffn_kernel, final_norm=final_norm),
        grid=(n // tm, dff // tf),
        in_specs=[
            pl.BlockSpec((tm, d), lambda i, j: (i, 0)),
            pl.BlockSpec((1, d), lambda i, j: (0, 0)),
            pl.BlockSpec((d, tf), lambda i, j: (0, j)),
            pl.BlockSpec((d, tf), lambda i, j: (0, j)),
            pl.BlockSpec((tf, d), lambda i, j: (j, 0)),
            pl.BlockSpec((1, d), lambda i, j: (0, 0)),
        ],
        out_specs=pl.BlockSpec((tm, d), lambda i, j: (i, 0)),
        out_shape=jax.ShapeDtypeStruct((n, d), F32),
        scratch_shapes=[pltpu.VMEM((tm, d), BF16), pltpu.VMEM((tm, d), F32)],
        compiler_params=pltpu.CompilerParams(
            dimension_semantics=("parallel", "arbitrary"), vmem_limit_bytes=VMEM_LIMIT_BYTES),
        name="swiglu_ffn",
    )(x2d, g, w_gate, w_up, w_out, fg)


def _pick_tile(n, want):
    t = min(n, want)
    while n % t:
        t //= 2
    return t


def kernel(x, rel_bias, mix_norm, w_in, w_attn_out, dw_kernel, dw_bias, conv_norm_g, conv_norm_b,
           w_conv_out, w_mix_out, ffn_norm, w_ffn_in, w_ffn_out, final_norm):
    b, s, d = x.shape
    depth = w_in.shape[0]
    n = b * s
    conv_ch = d // 2
    d_ff = w_ffn_out.shape[1]
    topk = min(TOPK_MAX, s // 4)
    tile = ATTN_TILE
    nq = s // tile

    sizes = (ATTN_WIDTH, ATTN_WIDTH, ATTN_WIDTH, N_IDX_HEADS * IDX_DIM, IDX_DIM, N_IDX_HEADS,
             2 * conv_ch, d, d)
    offs = np.concatenate([[0], np.cumsum(sizes)])
    o_ki, o_wi, o_u, o_g = offs[4], offs[5], offs[6], offs[7]

    bias = _bias_tiles(rel_bias.astype(F32), tile)

    x2d = x.reshape(n, d).astype(F32)
    for l in range(depth):
        w = w_in[l]
        wa = w[:, :o_ki].astype(BF16)
        w_ki = w[:, o_ki:o_wi]
        w_wi = w[:, o_wi:o_u]
        wkw = jnp.concatenate(
            [w_ki, w_ki, w_wi, jnp.zeros((d, LANES - N_IDX_HEADS), w.dtype)], axis=1).astype(BF16)
        wu = w[:, o_u:o_g].astype(BF16)
        wg = w[:, o_g:].astype(BF16)

        q, k, v, qi, ki2, wi, u, gg = _inproj(
            x2d, mix_norm[l].reshape(1, d).astype(F32), wa, wkw, wu, wg, tm=_pick_tile(n, 512))

        vt4 = v.reshape(b, nq, tile, ATTN_WIDTH).transpose(0, 1, 3, 2)
        wit = wi[:, :N_IDX_HEADS].reshape(b, s, N_IDX_HEADS).transpose(0, 2, 1)
        qt = q.reshape(b, s, ATTN_WIDTH).transpose(0, 2, 1)
        qit = qi.reshape(b, s, ATTN_WIDTH).transpose(0, 2, 1)
        attn = _attention(qt, qit, wit, k.reshape(b, s, -1), vt4,
                          ki2.reshape(b, s, -1), bias, tile=tile, topk=topk)

        conv = _conformer_conv(
            u.reshape(b, s, -1), dw_kernel[l].astype(F32), dw_bias[l].reshape(1, -1).astype(F32),
            conv_norm_g[l].reshape(1, -1).astype(F32), conv_norm_b[l].reshape(1, -1).astype(F32),
            tc=_pick_tile(s, 512))

        x2d = _mix(x2d, attn.reshape(n, -1), conv.reshape(n, -1), gg,
                   w_attn_out[l].astype(BF16), w_conv_out[l].astype(BF16), w_mix_out[l].astype(BF16),
                   tm=_pick_tile(n, 512))

        wf = w_ffn_in[l]
        x2d = _ffn(x2d, ffn_norm[l].reshape(1, d).astype(F32),
                   wf[:, :d_ff].astype(BF16), wf[:, d_ff:].astype(BF16), w_ffn_out[l].astype(BF16),
                   final_norm.reshape(1, d).astype(F32),
                   tm=_pick_tile(n, 1024), tf=256, final_norm=(l == depth - 1))
    return x2d.reshape(b, s, d).astype(x.dtype)
```

```python
import functools
import math

import numpy as np
import jax
import jax.numpy as jnp
from jax import lax
from jax.experimental import pallas as pl
from jax.experimental.pallas import tpu as pltpu

F32 = jnp.float32
BF16 = jnp.bfloat16
I32 = jnp.int32
I16 = jnp.int16

N_HEADS = 8
HEAD_DIM = 64
ATTN_WIDTH = N_HEADS * HEAD_DIM
N_IDX_HEADS = 8
IDX_DIM = 64
TOPK_MAX = 256
CONV_WIDTH = 31
N_BUCKETS = 32
MAX_DISTANCE = 128
EPS = 1e-6

LANES = 128
SUBLANES = 8
BF16_ROWS = 16
VMEM_LIMIT_BYTES = 56 * 1024 * 1024
LOG2E = math.log2(math.e)
NEG_BIG = -1e30
INT_MIN = -(2 ** 31)
I16_MIN = -(2 ** 15)
I16_MAX = 2 ** 15 - 1

ATTN_TILE = 256
V_ROWS = HEAD_DIM + BF16_ROWS
SCAN_GROUP = 4
CONV_HALO = 32


def _sigmoid(x):
    return 1.0 / (1.0 + jnp.exp(-x))


def _t5_bucket_ranges():
    max_exact = N_BUCKETS // 2
    buckets = []
    for n in range(0, 4 * MAX_DISTANCE):
        if n < max_exact:
            buckets.append(n)
        else:
            b = max_exact + int(math.log(n / max_exact) / math.log(MAX_DISTANCE / max_exact)
                                * (N_BUCKETS - max_exact))
            buckets.append(min(b, N_BUCKETS - 1))
    ranges = []
    for b in range(N_BUCKETS - 1):
        ns = [n for n, bb in enumerate(buckets) if bb == b]
        ranges.append((min(ns), max(ns)))
    return ranges


_BUCKET_RANGES = _t5_bucket_ranges()
_BIAS_REACH = _BUCKET_RANGES[-1][1] + 1


def _inproj_kernel(x_ref, g_ref, wa_ref, wkw_ref, wu_ref, wg_ref,
                   q_ref, k_ref, v_ref, qi_ref, ki_ref, wi_ref, u_ref, gg_ref, *, q_scale):
    x = x_ref[...]
    ms = jnp.mean(x * x, axis=-1, keepdims=True)
    h = ((x * lax.rsqrt(ms + EPS)) * g_ref[...]).astype(BF16)

    def mm(w):
        return jnp.dot(h, w, preferred_element_type=F32)

    w = ATTN_WIDTH
    q_ref[...] = (mm(wa_ref[:, 0:w]) * q_scale).astype(BF16)
    k_ref[...] = mm(wa_ref[:, w:2 * w]).astype(BF16)
    v_ref[...] = mm(wa_ref[:, 2 * w:3 * w]).astype(BF16)
    qi_ref[...] = mm(wa_ref[:, 3 * w:4 * w]).astype(BF16)
    kw = mm(wkw_ref[...])
    ki_ref[...] = kw[:, :LANES].astype(BF16)
    wi_ref[...] = kw[:, LANES:]
    u_ref[...] = mm(wu_ref[...])
    gg_ref[...] = mm(wg_ref[...])


def _inproj(x2d, g, wa, wkw, wu, wg, *, tm):
    n, d = x2d.shape
    grid = (n // tm,)
    row = lambda i: (i, 0)
    const = lambda i: (0, 0)
    out_shape = (
        jax.ShapeDtypeStruct((n, ATTN_WIDTH), BF16),
        jax.ShapeDtypeStruct((n, ATTN_WIDTH), BF16),
        jax.ShapeDtypeStruct((n, ATTN_WIDTH), BF16),
        jax.ShapeDtypeStruct((n, ATTN_WIDTH), BF16),
        jax.ShapeDtypeStruct((n, LANES), BF16),
        jax.ShapeDtypeStruct((n, LANES), F32),
        jax.ShapeDtypeStruct((n, wu.shape[1]), F32),
        jax.ShapeDtypeStruct((n, wg.shape[1]), F32),
    )
    return pl.pallas_call(
        functools.partial(_inproj_kernel, q_scale=HEAD_DIM ** -0.5 * LOG2E),
        grid=grid,
        in_specs=[
            pl.BlockSpec((tm, d), row),
            pl.BlockSpec((1, d), const),
            pl.BlockSpec(wa.shape, const),
            pl.BlockSpec(wkw.shape, const),
            pl.BlockSpec(wu.shape, const),
            pl.BlockSpec(wg.shape, const),
        ],
        out_specs=[pl.BlockSpec((tm, s.shape[1]), row) for s in out_shape],
        out_shape=out_shape,
        compiler_params=pltpu.CompilerParams(
            dimension_semantics=("parallel",), vmem_limit_bytes=VMEM_LIMIT_BYTES),
        name="inproj",
    )(x2d, g, wa, wkw, wu, wg)


def _bias_kernel(tab_ref, out_ref, *, tile):
    r = lax.broadcasted_iota(I32, (tile, tile), 0)
    c = lax.broadcasted_iota(I32, (tile, tile), 1)
    for w in range(2):
        dist = c - r + w * tile
        for h in range(N_HEADS):
            far = tab_ref[N_BUCKETS - 1, h]
            val = jnp.zeros((tile, tile), F32)
            for b, (lo, hi) in enumerate(_BUCKET_RANGES):
                hit = (dist == lo) if lo == hi else ((dist >= lo) & (dist <= hi))
                val = jnp.where(hit, (tab_ref[b, h] - far) * LOG2E, val)
            out_ref[w, h] = val


def _bias_tiles(rel_bias, tile):
    return pl.pallas_call(
        functools.partial(_bias_kernel, tile=tile),
        in_specs=[pl.BlockSpec(memory_space=pltpu.SMEM)],
        out_specs=pl.BlockSpec(memory_space=pltpu.VMEM),
        out_shape=jax.ShapeDtypeStruct((2, N_HEADS, tile, tile), F32),
        name="bias_tiles",
    )(rel_bias)


def _order_key(x):
    return x ^ ((x >> 31) & 0x7FFFFFFF)


def _attn_kernel(qt_ref, qit_ref, wi_ref, k_ref, vt_ref, ki_ref, bias_ref, o_ref,
                 hi_scr, lo_scr, mask_scr, s_scr, acc_scr, m_scr, alpha_scr,
                 qpad_scr, qipad_scr, *, tile, topk, seq_bits, wi_scale):
    t = tile
    i = pl.program_id(1)
    n_chunks = i + 1
    n_groups = (n_chunks + SCAN_GROUP - 1) // SCAN_GROUP

    def fold(x, op):
        return op(x.reshape(t // SUBLANES, SUBLANES, t), axis=0)

    def rows16(r):
        return slice(BF16_ROWS * r, BF16_ROWS * (r + 1))

    def to16(x):
        return jnp.broadcast_to(x, (BF16_ROWS, t)).astype(I16)

    feat = lax.broadcasted_iota(I32, (LANES, t), 0)
    for h in range(N_HEADS):
        pair = slice(LANES * (h // 2), LANES * (h // 2) + LANES)
        lo = HEAD_DIM * (h % 2)
        in_head = (feat >= lo) & (feat < lo + HEAD_DIM)
        qpad_scr[h] = jnp.where(in_head, qt_ref[0, pair, :], 0).astype(BF16)
        qipad_scr[h] = jnp.where(in_head, qit_ref[0, pair, :], 0).astype(BF16)

    w_all = wi_ref[0] * wi_scale
    row = lax.broadcasted_iota(I32, (t, t), 0)
    col = lax.broadcasted_iota(I32, (t, t), 1)

    def score_chunk(j, slot, diagonal):
        kij = ki_ref[0, pl.ds(pl.multiple_of(j * t, t), t), :]
        for hp in range(N_IDX_HEADS // 2):
            ha, hb = 2 * hp, 2 * hp + 1
            la = jnp.dot(kij, qipad_scr[ha], preferred_element_type=F32)
            lb = jnp.dot(kij, qipad_scr[hb], preferred_element_type=F32)
            term = (jnp.maximum(la, 0.0) * w_all[ha:ha + 1, :]
                    + jnp.maximum(lb, 0.0) * w_all[hb:hb + 1, :])
            if hp == 0:
                s_scr[slot] = term
            elif hp < N_IDX_HEADS // 2 - 1:
                s_scr[slot] = s_scr[slot] + term
            else:
                score = s_scr[slot] + term
                score = jnp.where(score == 0.0, 0.0, score)
                skey = _order_key(pltpu.bitcast(score, I32))
                if diagonal:
                    skey = jnp.where(row <= col, skey, INT_MIN)
                hi_scr[j] = (skey >> 16).astype(I16)
                lo_scr[j] = ((skey & 0xFFFF) + I16_MIN).astype(I16)

    def score_pair(p, carry):
        score_chunk(2 * p, 0, False)
        score_chunk(2 * p + 1, 1, False)
        return carry

    lax.fori_loop(0, i // 2, score_pair, 0)

    @pl.when(i % 2 == 1)
    def _():
        score_chunk(i - 1, 0, False)

    score_chunk(i, 1, True)

    def pad_chunk(j, carry):
        hi_scr[j] = jnp.full((t, t), I16_MIN, I16)
        lo_scr[j] = jnp.full((t, t), I16_MIN, I16)
        return carry

    lax.fori_loop(n_chunks, n_groups * SCAN_GROUP, pad_chunk, 0)

    def count_ge16(plane, cands):
        cands16 = [to16(c) for c in cands]

        def body(g, parts):
            parts = list(parts)
            for u in range(SCAN_GROUP):
                j = g * SCAN_GROUP + u
                for ci, c16 in enumerate(cands16):
                    acc = parts[ci]
                    for r in range(t // BF16_ROWS):
                        acc = acc + jnp.where(plane[j, rows16(r), :] >= c16, jnp.int16(1), jnp.int16(0))
                    parts[ci] = acc
            return tuple(parts)

        zero = jnp.zeros((BF16_ROWS, t), I16)
        parts = lax.fori_loop(0, n_groups, body, tuple(zero for _ in cands))
        return [jnp.sum(p.astype(I32), axis=0, keepdims=True) for p in parts]

    def select16(plane, want):
        def bit_step(it, cur_u):
            cand_u = cur_u | (jnp.int32(1) << (15 - it))
            cnt, = count_ge16(plane, [cand_u + I16_MIN])
            return jnp.where(cnt >= want, cand_u, cur_u)
        return lax.fori_loop(0, 16, bit_step, jnp.zeros((1, t), I32)) + I16_MIN

    th = select16(hi_scr, jnp.full((1, t), topk, I32))
    c_hi_ge, c_above = count_ge16(hi_scr, [th, jnp.minimum(th + 1, I16_MAX)])
    th16 = to16(th)

    def low_candidates(j, carry):
        for r in range(t // BF16_ROWS):
            lo_scr[j, rows16(r), :] = jnp.where(hi_scr[j, rows16(r), :] == th16,
                                                lo_scr[j, rows16(r), :], jnp.int16(I16_MIN))
        return carry

    lax.fori_loop(0, n_groups * SCAN_GROUP, low_candidates, 0)

    tl = select16(lo_scr, topk - c_above)
    c_lo_ge, c_lo_gt = count_ge16(lo_scr, [tl, jnp.minimum(tl + 1, I16_MAX)])
    c_lo_gt = jnp.where(tl == I16_MAX, 0, c_lo_gt)
    c_ge = jnp.where(tl == I16_MIN, c_hi_ge, c_above + c_lo_ge)
    need = topk - (c_above + c_lo_gt)
    has_thr = th > I16_MIN
    tie_split = (c_ge > topk) & has_thr
    any_split = jnp.max(jnp.where(tie_split, 1, 0)) > 0

    @pl.when(jnp.logical_not(any_split))
    def _():
        tl16 = to16(tl)
        keep = jnp.zeros((BF16_ROWS, t), BF16)
        drop = jnp.full((BF16_ROWS, t), NEG_BIG, BF16)
        tie = jnp.where(to16(jnp.where(has_thr, 1, 0)) > 0, keep, drop)

        def mask_chunk(j, carry):
            for r in range(t // BF16_ROWS):
                hs = hi_scr[j, rows16(r), :]
                ls = lo_scr[j, rows16(r), :]
                at_th = jnp.where(ls > tl16, keep, jnp.where(ls == tl16, tie, drop))
                mb = jnp.where(hs > th16, keep, jnp.where(hs == th16, at_th, drop))
                mask_scr[j, rows16(r), :] = mb.astype(F32)
            return carry

        lax.fori_loop(0, n_chunks, mask_chunk, 0)

    @pl.when(any_split)
    def _():
        def tie_bit(it, cur):
            cand = cur | (jnp.int32(1) << (seq_bits - 1 - it))

            def body(j, part):
                idx = j * t + row
                is_tie = (hi_scr[j].astype(I32) == th) & (lo_scr[j].astype(I32) == tl)
                hit = jnp.where(is_tie, jnp.where(idx < cand, 1, 0), 0)
                return part + fold(hit, jnp.sum)

            part = lax.fori_loop(0, n_chunks, body, jnp.zeros((SUBLANES, t), I32))
            cnt = jnp.sum(part, axis=0, keepdims=True)
            return jnp.where(cnt < need, cand, cur)

        p = lax.fori_loop(0, seq_bits, tie_bit, jnp.zeros((1, t), I32))
        jcut = jnp.where(tie_split, p, jnp.where(has_thr, jnp.int32(2 ** 30), jnp.int32(-1)))

        def mask_chunk(j, carry):
            hs = hi_scr[j].astype(I32)
            ls = lo_scr[j].astype(I32)
            tie_ok = jnp.where(j * t + row <= jcut, 0.0, NEG_BIG)
            at_th = jnp.where(ls > tl, 0.0, jnp.where(ls == tl, tie_ok, NEG_BIG))
            mask_scr[j] = jnp.where(hs > th, 0.0, jnp.where(hs == th, at_th, NEG_BIG))
            return carry

        lax.fori_loop(0, n_chunks, mask_chunk, 0)

    m_scr[...] = jnp.full(m_scr.shape, -jnp.inf, F32)
    acc_scr[...] = jnp.zeros(acc_scr.shape, F32)

    def attend(j, slot0, bias_sel, live=None):
        for h in range(N_HEADS):
            pair = slice(LANES * (h // 2), LANES * (h // 2) + LANES)
            kc = k_ref[0, pl.ds(pl.multiple_of(j * t, t), t), pair]
            mb = mask_scr[j]
            if live is not None:
                mb = jnp.where(live, mb, NEG_BIG)
            s = jnp.dot(kc, qpad_scr[h], preferred_element_type=F32) + mb
            if bias_sel is not None:
                s = s + bias_ref[bias_sel, h]
            s_scr[slot0 + h] = s
            m_old = m_scr[h:h + 1, :]
            m_new = jnp.maximum(m_old, jnp.max(s, axis=0, keepdims=True))
            alpha_scr[slot0 + h:slot0 + h + 1, :] = jnp.exp2(m_old - m_new)
            m_scr[h:h + 1, :] = m_new
        for h in range(N_HEADS):
            rows = slice(V_ROWS * h, V_ROWS * (h + 1))
            p = jnp.exp2((s_scr[slot0 + h] - m_scr[h:h + 1, :]).astype(BF16))
            pv = jnp.dot(vt_ref[0, j, rows, :], p, preferred_element_type=F32)
            acc_scr[rows, :] = alpha_scr[slot0 + h:slot0 + h + 1, :] * acc_scr[rows, :] + pv

    n_far = jnp.maximum(i - 1, 0)

    def far_pair(p, carry):
        attend(2 * p, 0, None)
        attend(2 * p + 1, N_HEADS, None)
        return carry

    lax.fori_loop(0, n_far // 2, far_pair, 0)

    @pl.when(n_far % 2 == 1)
    def _():
        attend(n_far - 1, 0, None)

    attend(jnp.maximum(i - 1, 0), 0, 1, live=i >= 1)
    attend(i, N_HEADS, 0)

    outs = []
    for h in range(N_HEADS):
        base = V_ROWS * h
        inv_l = 1.0 / acc_scr[base + HEAD_DIM:base + HEAD_DIM + 1, :]
        outs.append(acc_scr[base:base + HEAD_DIM, :] * inv_l)
    o_ref[0] = jnp.concatenate(outs, axis=0).T.astype(BF16)


def _attention(qt, qit, wit, k, vt4, ki2, bias, *, tile, topk):
    b, s, _ = k.shape
    nq = s // tile
    assert tile >= topk and tile >= _BIAS_REACH and s % tile == 0 and (s & (s - 1)) == 0
    assert nq % SCAN_GROUP == 0
    seq_bits = s.bit_length() - 1
    kern = functools.partial(_attn_kernel, tile=tile, topk=topk, seq_bits=seq_bits,
                             wi_scale=(N_IDX_HEADS ** -0.5) * (IDX_DIM ** -0.5))
    resident = pl.Buffered(1)
    return pl.pallas_call(
        kern,
        grid=(b, nq),
        in_specs=[
            pl.BlockSpec((1, ATTN_WIDTH, tile), lambda bb, i: (bb, 0, i)),
            pl.BlockSpec((1, ATTN_WIDTH, tile), lambda bb, i: (bb, 0, i)),
            pl.BlockSpec((1, N_IDX_HEADS, tile), lambda bb, i: (bb, 0, i)),
            pl.BlockSpec((1, s, ATTN_WIDTH), lambda bb, i: (bb, 0, 0), pipeline_mode=resident),
            pl.BlockSpec((1, nq, N_HEADS * V_ROWS, tile), lambda bb, i: (bb, 0, 0, 0),
                         pipeline_mode=resident),
            pl.BlockSpec((1, s, LANES), lambda bb, i: (bb, 0, 0), pipeline_mode=resident),
            pl.BlockSpec((2, N_HEADS, tile, tile), lambda bb, i: (0, 0, 0, 0),
                         pipeline_mode=resident),
        ],
        out_specs=pl.BlockSpec((1, tile, ATTN_WIDTH), lambda bb, i: (bb, i, 0)),
        out_shape=jax.ShapeDtypeStruct((b, s, ATTN_WIDTH), BF16),
        scratch_shapes=[
            pltpu.VMEM((nq, tile, tile), I16),
            pltpu.VMEM((nq, tile, tile), I16),
            pltpu.VMEM((nq, tile, tile), F32),
            pltpu.VMEM((2 * N_HEADS, tile, tile), F32),
            pltpu.VMEM((N_HEADS * V_ROWS, tile), F32),
            pltpu.VMEM((N_HEADS, tile), F32),
            pltpu.VMEM((2 * N_HEADS, tile), F32),
            pltpu.VMEM((N_HEADS, LANES, tile), BF16),
            pltpu.VMEM((N_IDX_HEADS, LANES, tile), BF16),
        ],
        compiler_params=pltpu.CompilerParams(
            dimension_semantics=("arbitrary", "arbitrary"), vmem_limit_bytes=VMEM_LIMIT_BYTES),
        name="dsa_attention",
    )(qt, qit, wit, k, vt4, ki2, bias)


def _conv_kernel(u_ref, dw_ref, db_ref, g_ref, b_ref, o_ref, h_scr, *, tc, sub):
    ch = o_ref.shape[-1]
    halo = CONV_HALO

    @pl.when(pl.program_id(1) == 0)
    def _():
        h_scr[0:halo, :] = jnp.zeros((halo, ch), F32)

    u = u_ref[0]
    h_scr[halo:halo + tc, :] = u[:, :ch] * _sigmoid(u[:, ch:])
    first = halo - (CONV_WIDTH - 1)

    for r in range(tc // sub):
        base = r * sub
        acc = jnp.zeros((sub, ch), F32)
        for j in range(CONV_WIDTH):
            acc = acc + dw_ref[j:j + 1, :] * h_scr[base + first + j:base + first + j + sub, :]
        acc = acc + db_ref[...]
        mu = jnp.mean(acc, axis=-1, keepdims=True)
        cen = acc - mu
        var = jnp.mean(cen * cen, axis=-1, keepdims=True)
        y = cen * lax.rsqrt(var + EPS) * g_ref[...] + b_ref[...]
        o_ref[0, base:base + sub, :] = (y * _sigmoid(y)).astype(o_ref.dtype)
    h_scr[0:halo, :] = h_scr[tc:tc + halo, :]


def _conformer_conv(u, dw, db, g, bb, *, tc, sub=64):
    b, s, c2 = u.shape
    ch = c2 // 2
    const = lambda bi, i: (0, 0)
    return pl.pallas_call(
        functools.partial(_conv_kernel, tc=tc, sub=sub),
        grid=(b, s // tc),
        in_specs=[
            pl.BlockSpec((1, tc, c2), lambda bi, i: (bi, i, 0)),
            pl.BlockSpec(dw.shape, const),
            pl.BlockSpec((1, ch), const),
            pl.BlockSpec((1, ch), const),
            pl.BlockSpec((1, ch), const),
        ],
        out_specs=pl.BlockSpec((1, tc, ch), lambda bi, i: (bi, i, 0)),
        out_shape=jax.ShapeDtypeStruct((b, s, ch), BF16),
        scratch_shapes=[pltpu.VMEM((tc + CONV_HALO, ch), F32)],
        compiler_params=pltpu.CompilerParams(
            dimension_semantics=("arbitrary", "arbitrary"), vmem_limit_bytes=VMEM_LIMIT_BYTES),
        name="conformer_conv",
    )(u, dw, db, g, bb)


def _mix_kernel(x_ref, a_ref, c_ref, gg_ref, wa_ref, wc_ref, wm_ref, o_ref):
    d = x_ref.shape[-1]
    y_a = jnp.dot(a_ref[...], wa_ref[...], preferred_element_type=F32)
    y_b = jnp.dot(c_ref[...], wc_ref[...], preferred_element_type=F32)
    gg = gg_ref[...]
    merged = _sigmoid(gg[:, :d]) * y_a + _sigmoid(gg[:, d:]) * y_b
    o_ref[...] = x_ref[...] + jnp.dot(merged.astype(BF16), wm_ref[...], preferred_element_type=F32)


def _mix(x2d, attn, conv, gg, wa, wc, wm, *, tm):
    n, d = x2d.shape
    row = lambda i: (i, 0)
    const = lambda i: (0, 0)
    return pl.pallas_call(
        _mix_kernel,
        grid=(n // tm,),
        in_specs=[
            pl.BlockSpec((tm, d), row),
            pl.BlockSpec((tm, attn.shape[1]), row),
            pl.BlockSpec((tm, conv.shape[1]), row),
            pl.BlockSpec((tm, gg.shape[1]), row),
            pl.BlockSpec(wa.shape, const),
            pl.BlockSpec(wc.shape, const),
            pl.BlockSpec(wm.shape, const),
        ],
        out_specs=pl.BlockSpec((tm, d), row),
        out_shape=jax.ShapeDtypeStruct((n, d), F32),
        compiler_params=pltpu.CompilerParams(
            dimension_semantics=("parallel",), vmem_limit_bytes=VMEM_LIMIT_BYTES),
        name="gated_mix",
    )(x2d, attn, conv, gg, wa, wc, wm)


def _ffn_kernel(x_ref, g_ref, wg_ref, wu_ref, wo_ref, fg_ref, o_ref, h_scr, acc_scr, *, final_norm):
    j = pl.program_id(1)

    @pl.when(j == 0)
    def _():
        x = x_ref[...]
        ms = jnp.mean(x * x, axis=-1, keepdims=True)
        h_scr[...] = ((x * lax.rsqrt(ms + EPS)) * g_ref[...]).astype(BF16)
        acc_scr[...] = jnp.zeros(acc_scr.shape, F32)

    h = h_scr[...]
    gate = jnp.dot(h, wg_ref[...], preferred_element_type=F32)
    up = jnp.dot(h, wu_ref[...], preferred_element_type=F32)
    act = (gate * _sigmoid(gate) * up).astype(BF16)
    acc_scr[...] += jnp.dot(act, wo_ref[...], preferred_element_type=F32)

    @pl.when(j == pl.num_programs(1) - 1)
    def _():
        y = x_ref[...] + acc_scr[...]
        if final_norm:
            ms = jnp.mean(y * y, axis=-1, keepdims=True)
            y = (y * lax.rsqrt(ms + EPS)) * fg_ref[...]
        o_ref[...] = y


def _ffn(x2d, g, w_gate, w_up, w_out, fg, *, tm, tf, final_norm):
    n, d = x2d.shape
    dff = w_gate.shape[1]
    return pl.pallas_call(
        functools.partial(_ffn_kernel, final_norm=final_norm),
        grid=(n // tm, dff // tf),
        in_specs=[
            pl.BlockSpec((tm, d), lambda i, j: (i, 0)),
            pl.BlockSpec((1, d), lambda i, j: (0, 0)),
            pl.BlockSpec((d, tf), lambda i, j: (0, j)),
            pl.BlockSpec((d, tf), lambda i, j: (0, j)),
            pl.BlockSpec((tf, d), lambda i, j: (j, 0)),
            pl.BlockSpec((1, d), lambda i, j: (0, 0)),
        ],
        out_specs=pl.BlockSpec((tm, d), lambda i, j: (i, 0)),
        out_shape=jax.ShapeDtypeStruct((n, d), F32),
        scratch_shapes=[pltpu.VMEM((tm, d), BF16), pltpu.VMEM((tm, d), F32)],
        compiler_params=pltpu.CompilerParams(
            dimension_semantics=("parallel", "arbitrary"), vmem_limit_bytes=VMEM_LIMIT_BYTES),
        name="swiglu_ffn",
    )(x2d, g, w_gate, w_up, w_out, fg)


def _pick_tile(n, want):
    t = min(n, want)
    while n % t:
        t //= 2
    return t


def kernel(x, rel_bias, mix_norm, w_in, w_attn_out, dw_kernel, dw_bias, conv_norm_g, conv_norm_b,
           w_conv_out, w_mix_out, ffn_norm, w_ffn_in, w_ffn_out, final_norm):
    b, s, d = x.shape
    depth = w_in.shape[0]
    n = b * s
    conv_ch = d // 2
    d_ff = w_ffn_out.shape[1]
    topk = min(TOPK_MAX, s // 4)
    tile = ATTN_TILE
    nq = s // tile

    sizes = (ATTN_WIDTH, ATTN_WIDTH, ATTN_WIDTH, N_IDX_HEADS * IDX_DIM, IDX_DIM, N_IDX_HEADS,
             2 * conv_ch, d, d)
    offs = np.concatenate([[0], np.cumsum(sizes)])
    o_ki, o_wi, o_u, o_g = offs[4], offs[5], offs[6], offs[7]

    bias = _bias_tiles(rel_bias.astype(F32), tile)
    v_tail = jnp.zeros((b, nq, N_HEADS, V_ROWS - HEAD_DIM, tile), BF16).at[:, :, :, 0, :].set(1)

    x2d = x.reshape(n, d).astype(F32)
    for l in range(depth):
        w = w_in[l]
        wa = w[:, :o_ki].astype(BF16)
        w_ki = w[:, o_ki:o_wi]
        w_wi = w[:, o_wi:o_u]
        wkw = jnp.concatenate(
            [w_ki, w_ki, w_wi, jnp.zeros((d, LANES - N_IDX_HEADS), w.dtype)], axis=1).astype(BF16)
        wu = w[:, o_u:o_g].astype(BF16)
        wg = w[:, o_g:].astype(BF16)

        q, k, v, qi, ki2, wi, u, gg = _inproj(
            x2d, mix_norm[l].reshape(1, d).astype(F32), wa, wkw, wu, wg, tm=_pick_tile(n, 512))

        vt = v.reshape(b, nq, tile, N_HEADS, HEAD_DIM).transpose(0, 1, 3, 4, 2)
        vt4 = jnp.concatenate([vt, v_tail], axis=3).reshape(b, nq, N_HEADS * V_ROWS, tile)
        wit = wi[:, :N_IDX_HEADS].reshape(b, s, N_IDX_HEADS).transpose(0, 2, 1)
        qt = q.reshape(b, s, ATTN_WIDTH).transpose(0, 2, 1)
        qit = qi.reshape(b, s, ATTN_WIDTH).transpose(0, 2, 1)
        attn = _attention(qt, qit, wit, k.reshape(b, s, -1), vt4,
                          ki2.reshape(b, s, -1), bias, tile=tile, topk=topk)

        conv = _conformer_conv(
            u.reshape(b, s, -1), dw_kernel[l].astype(F32), dw_bias[l].reshape(1, -1).astype(F32),
            conv_norm_g[l].reshape(1, -1).astype(F32), conv_norm_b[l].reshape(1, -1).astype(F32),
            tc=_pick_tile(s, 512))

        x2d = _mix(x2d, attn.reshape(n, -1), conv.reshape(n, -1), gg,
                   w_attn_out[l].astype(BF16), w_conv_out[l].astype(BF16), w_mix_out[l].astype(BF16),
                   tm=_pick_tile(n, 512))

        wf = w_ffn_in[l]
        x2d = _ffn(x2d, ffn_norm[l].reshape(1, d).astype(F32),
                   wf[:, :d_ff].astype(BF16), wf[:, d_ff:].astype(BF16), w_ffn_out[l].astype(BF16),
                   final_norm.reshape(1, d).astype(F32),
                   tm=_pick_tile(n, 1024), tf=256, final_norm=(l == depth - 1))
    return x2d.reshape(b, s, d).astype(x.dtype)
```

```python
import functools
import math

import numpy as np
import jax
import jax.numpy as jnp
from jax import lax
from jax.experimental import pallas as pl
from jax.experimental.pallas import tpu as pltpu

F32 = jnp.float32
BF16 = jnp.bfloat16
I32 = jnp.int32
I16 = jnp.int16

N_HEADS = 8
HEAD_DIM = 64
ATTN_WIDTH = N_HEADS * HEAD_DIM
N_IDX_HEADS = 8
IDX_DIM = 64
TOPK_MAX = 256
CONV_WIDTH = 31
N_BUCKETS = 32
MAX_DISTANCE = 128
EPS = 1e-6

LANES = 128
SUBLANES = 8
BF16_ROWS = 16
VMEM_LIMIT_BYTES = 56 * 1024 * 1024
LOG2E = math.log2(math.e)
NEG_BIG = -1e30
INT_MIN = -(2 ** 31)
I16_MIN = -(2 ** 15)
I16_MAX = 2 ** 15 - 1

ATTN_TILE = 256
V_ROWS = HEAD_DIM + BF16_ROWS
SCAN_GROUP = 4
CONV_HALO = 32


def _sigmoid(x):
    return 1.0 / (1.0 + jnp.exp(-x))


def _t5_bucket_ranges():
    max_exact = N_BUCKETS // 2
    buckets = []
    for n in range(0, 4 * MAX_DISTANCE):
        if n < max_exact:
            buckets.append(n)
        else:
            b = max_exact + int(math.log(n / max_exact) / math.log(MAX_DISTANCE / max_exact)
                                * (N_BUCKETS - max_exact))
            buckets.append(min(b, N_BUCKETS - 1))
    ranges = []
    for b in range(N_BUCKETS - 1):
        ns = [n for n, bb in enumerate(buckets) if bb == b]
        ranges.append((min(ns), max(ns)))
    return ranges


_BUCKET_RANGES = _t5_bucket_ranges()
_BIAS_REACH = _BUCKET_RANGES[-1][1] + 1


def _inproj_kernel(x_ref, g_ref, wa_ref, wkw_ref, wu_ref, wg_ref,
                   q_ref, k_ref, v_ref, qi_ref, ki_ref, wi_ref, u_ref, gg_ref, *, q_scale):
    x = x_ref[...]
    ms = jnp.mean(x * x, axis=-1, keepdims=True)
    h = ((x * lax.rsqrt(ms + EPS)) * g_ref[...]).astype(BF16)

    def mm(w):
        return jnp.dot(h, w, preferred_element_type=F32)

    w = ATTN_WIDTH
    q_ref[...] = (mm(wa_ref[:, 0:w]) * q_scale).astype(BF16)
    k_ref[...] = mm(wa_ref[:, w:2 * w]).astype(BF16)
    v_ref[...] = mm(wa_ref[:, 2 * w:3 * w]).astype(BF16)
    qi_ref[...] = mm(wa_ref[:, 3 * w:4 * w]).astype(BF16)
    kw = mm(wkw_ref[...])
    ki_ref[...] = kw[:, :LANES].astype(BF16)
    wi_ref[...] = kw[:, LANES:]
    u_ref[...] = mm(wu_ref[...])
    gg_ref[...] = mm(wg_ref[...])


def _inproj(x2d, g, wa, wkw, wu, wg, *, tm):
    n, d = x2d.shape
    grid = (n // tm,)
    row = lambda i: (i, 0)
    const = lambda i: (0, 0)
    out_shape = (
        jax.ShapeDtypeStruct((n, ATTN_WIDTH), BF16),
        jax.ShapeDtypeStruct((n, ATTN_WIDTH), BF16),
        jax.ShapeDtypeStruct((n, ATTN_WIDTH), BF16),
        jax.ShapeDtypeStruct((n, ATTN_WIDTH), BF16),
        jax.ShapeDtypeStruct((n, LANES), BF16),
        jax.ShapeDtypeStruct((n, LANES), F32),
        jax.ShapeDtypeStruct((n, wu.shape[1]), F32),
        jax.ShapeDtypeStruct((n, wg.shape[1]), F32),
    )
    return pl.pallas_call(
        functools.partial(_inproj_kernel, q_scale=HEAD_DIM ** -0.5 * LOG2E),
        grid=grid,
        in_specs=[
            pl.BlockSpec((tm, d), row),
            pl.BlockSpec((1, d), const),
            pl.BlockSpec(wa.shape, const),
            pl.BlockSpec(wkw.shape, const),
            pl.BlockSpec(wu.shape, const),
            pl.BlockSpec(wg.shape, const),
        ],
        out_specs=[pl.BlockSpec((tm, s.shape[1]), row) for s in out_shape],
        out_shape=out_shape,
        compiler_params=pltpu.CompilerParams(
            dimension_semantics=("parallel",), vmem_limit_bytes=VMEM_LIMIT_BYTES),
        name="inproj",
    )(x2d, g, wa, wkw, wu, wg)


def _bias_kernel(tab_ref, out_ref, *, tile):
    r = lax.broadcasted_iota(I32, (tile, tile), 0)
    c = lax.broadcasted_iota(I32, (tile, tile), 1)
    for w in range(2):
        dist = c - r + w * tile
        for h in range(N_HEADS):
            far = tab_ref[N_BUCKETS - 1, h]
            val = jnp.zeros((tile, tile), F32)
            for b, (lo, hi) in enumerate(_BUCKET_RANGES):
                hit = (dist == lo) if lo == hi else ((dist >= lo) & (dist <= hi))
                val = jnp.where(hit, (tab_ref[b, h] - far) * LOG2E, val)
            out_ref[w, h] = val


def _bias_tiles(rel_bias, tile):
    return pl.pallas_call(
        functools.partial(_bias_kernel, tile=tile),
        in_specs=[pl.BlockSpec(memory_space=pltpu.SMEM)],
        out_specs=pl.BlockSpec(memory_space=pltpu.VMEM),
        out_shape=jax.ShapeDtypeStruct((2, N_HEADS, tile, tile), F32),
        name="bias_tiles",
    )(rel_bias)


def _order_key(x):
    return x ^ ((x >> 31) & 0x7FFFFFFF)


def _attn_kernel(qt_ref, qit_ref, wi_ref, k_ref, vt_ref, ki_ref, bias_ref, o_ref,
                 hi_scr, lo_scr, mask_scr, s_scr, acc_scr, m_scr, alpha_scr, tot_scr, col_scr,
                 qpad_scr, qipad_scr, *, tile, topk, tile_bits, wi_scale):
    t = tile
    i = pl.program_id(1)
    n_chunks = i + 1
    n_groups = n_chunks // SCAN_GROUP

    def fold(x, op):
        return op(x.reshape(t // SUBLANES, SUBLANES, t), axis=0)

    def rows16(r):
        return slice(BF16_ROWS * r, BF16_ROWS * (r + 1))

    def to16(x):
        return jnp.broadcast_to(x, (BF16_ROWS, t)).astype(I16)

    feat = lax.broadcasted_iota(I32, (LANES, t), 0)
    for h in range(N_HEADS):
        pair = slice(LANES * (h // 2), LANES * (h // 2) + LANES)
        lo = HEAD_DIM * (h % 2)
        in_head = (feat >= lo) & (feat < lo + HEAD_DIM)
        qpad_scr[h] = jnp.where(in_head, qt_ref[0, pair, :], 0).astype(BF16)
        qipad_scr[h] = jnp.where(in_head, qit_ref[0, pair, :], 0).astype(BF16)

    w_all = wi_ref[0] * wi_scale
    row = lax.broadcasted_iota(I32, (t, t), 0)
    col = lax.broadcasted_iota(I32, (t, t), 1)

    def score_chunk(j, slot, diagonal):
        kij = ki_ref[0, pl.ds(pl.multiple_of(j * t, t), t), :]
        for hp in range(N_IDX_HEADS // 2):
            ha, hb = 2 * hp, 2 * hp + 1
            la = jnp.dot(kij, qipad_scr[ha], preferred_element_type=F32)
            lb = jnp.dot(kij, qipad_scr[hb], preferred_element_type=F32)
            term = (jnp.maximum(la, 0.0) * w_all[ha:ha + 1, :]
                    + jnp.maximum(lb, 0.0) * w_all[hb:hb + 1, :])
            if hp == 0:
                s_scr[slot] = term
            elif hp < N_IDX_HEADS // 2 - 1:
                s_scr[slot] = s_scr[slot] + term
            else:
                score = s_scr[slot] + term
                score = jnp.where(score == 0.0, 0.0, score)
                skey = _order_key(pltpu.bitcast(score, I32))
                if diagonal:
                    skey = jnp.where(row <= col, skey, INT_MIN)
                hi_scr[j] = (skey >> 16).astype(I16)
                lo_scr[j] = ((skey & 0xFFFF) + I16_MIN).astype(I16)

    def score_pair(p, carry):
        score_chunk(2 * p, 0, False)
        score_chunk(2 * p + 1, 1, False)
        return carry

    lax.fori_loop(0, i // 2, score_pair, 0)

    @pl.when(i % 2 == 1)
    def _():
        score_chunk(i - 1, 0, False)

    score_chunk(i, 1, True)

    def count_ge16(plane, cands):
        cands16 = [to16(c) for c in cands]

        def scan_chunk(j, parts):
            parts = list(parts)
            for ci, c16 in enumerate(cands16):
                acc = parts[ci]
                for r in range(t // BF16_ROWS):
                    acc = acc + jnp.where(plane[j, rows16(r), :] >= c16, jnp.int16(1), jnp.int16(0))
                parts[ci] = acc
            return tuple(parts)

        def scan_group(g, parts):
            for u in range(SCAN_GROUP):
                parts = scan_chunk(g * SCAN_GROUP + u, parts)
            return parts

        zero = jnp.zeros((BF16_ROWS, t), I16)
        parts = lax.fori_loop(0, n_groups, scan_group, tuple(zero for _ in cands))
        parts = lax.fori_loop(n_groups * SCAN_GROUP, n_chunks, scan_chunk, parts)
        return [jnp.sum(p.astype(I32), axis=0, keepdims=True) for p in parts]

    def select16(plane, want):
        def bit_step(it, cur_u):
            cand_u = cur_u | (jnp.int32(1) << (15 - it))
            cnt, = count_ge16(plane, [cand_u + I16_MIN])
            return jnp.where(cnt >= want, cand_u, cur_u)
        return lax.fori_loop(0, 16, bit_step, jnp.zeros((1, t), I32)) + I16_MIN

    th = select16(hi_scr, jnp.full((1, t), topk, I32))
    c_hi_ge, c_above = count_ge16(hi_scr, [th, jnp.minimum(th + 1, I16_MAX)])
    th16 = to16(th)

    def low_candidates(j, carry):
        for r in range(t // BF16_ROWS):
            lo_scr[j, rows16(r), :] = jnp.where(hi_scr[j, rows16(r), :] == th16,
                                                lo_scr[j, rows16(r), :], jnp.int16(I16_MIN))
        return carry

    lax.fori_loop(0, n_chunks, low_candidates, 0)

    tl = select16(lo_scr, topk - c_above)
    c_lo_ge, c_lo_gt = count_ge16(lo_scr, [tl, jnp.minimum(tl + 1, I16_MAX)])
    c_lo_gt = jnp.where(tl == I16_MAX, 0, c_lo_gt)
    c_ge = jnp.where(tl == I16_MIN, c_hi_ge, c_above + c_lo_ge)
    need = topk - (c_above + c_lo_gt)
    has_thr = th > I16_MIN
    tie_split = (c_ge > topk) & has_thr
    any_split = jnp.max(jnp.where(tie_split, 1, 0)) > 0

    tl16 = to16(tl)
    keep = jnp.zeros((BF16_ROWS, t), BF16)
    drop = jnp.full((BF16_ROWS, t), NEG_BIG, BF16)
    one16 = jnp.ones((BF16_ROWS, t), I16)
    zero16 = jnp.zeros((BF16_ROWS, t), I16)

    def lanes16(pred):
        return to16(jnp.where(pred, 1, 0)) > 0

    @pl.when(jnp.logical_not(any_split))
    def _():
        tie = jnp.where(lanes16(has_thr), keep, drop)

        def mask_chunk(j, carry):
            for r in range(t // BF16_ROWS):
                hs = hi_scr[j, rows16(r), :]
                ls = lo_scr[j, rows16(r), :]
                at_th = jnp.where(ls > tl16, keep, jnp.where(ls == tl16, tie, drop))
                mb = jnp.where(hs > th16, keep, jnp.where(hs == th16, at_th, drop))
                mask_scr[j, rows16(r), :] = mb.astype(F32)
            return carry

        lax.fori_loop(0, n_chunks, mask_chunk, 0)

    @pl.when(any_split)
    def _():
        row16 = lax.broadcasted_iota(I32, (BF16_ROWS, t), 0).astype(I16)

        def tie_tile(j, r):
            hs = hi_scr[j, rows16(r), :]
            ls = lo_scr[j, rows16(r), :]
            return jnp.where(hs == th16, jnp.where(ls == tl16, one16, zero16), zero16)

        def chunk_total(j, carry):
            acc = zero16
            for r in range(t // BF16_ROWS):
                acc = acc + tie_tile(j, r)
            tot = jnp.sum(acc.astype(I32), axis=0, keepdims=True)
            tot_scr[j] = jnp.broadcast_to(tot, (SUBLANES, t))
            return carry

        lax.fori_loop(0, n_chunks, chunk_total, 0)

        def locate_chunk(j, carry):
            cum, n_before, ties_before = carry
            tot = tot_scr[j][0:1, :]
            cum = cum + tot
            before = cum < need
            return cum, n_before + jnp.where(before, 1, 0), ties_before + jnp.where(before, tot, 0)

        zero1 = jnp.zeros((1, t), I32)
        _, cut_chunk, ties_before = lax.fori_loop(0, n_chunks, locate_chunk, (zero1, zero1, zero1))

        col_scr[...] = jnp.zeros(col_scr.shape, I16)

        def gather_chunk(j, carry):
            here = lanes16(cut_chunk == j)
            for r in range(t // BF16_ROWS):
                col_scr[rows16(r), :] = jnp.where(here, tie_tile(j, r), col_scr[rows16(r), :])
            return carry

        lax.fori_loop(0, n_chunks, gather_chunk, 0)

        wanted = need - ties_before

        def row_bit(it, cur):
            cand = cur | (jnp.int32(1) << (tile_bits - 1 - it))
            cand16 = to16(cand)
            acc = zero16
            for r in range(t // BF16_ROWS):
                acc = acc + jnp.where(row16 + jnp.int16(BF16_ROWS * r) < cand16,
                                      col_scr[rows16(r), :], zero16)
            cnt = jnp.sum(acc.astype(I32), axis=0, keepdims=True)
            return jnp.where(cnt < wanted, cand, cur)

        cut_row16 = to16(lax.fori_loop(0, tile_bits, row_bit, zero1))
        cut_chunk = jnp.where(tie_split, cut_chunk, jnp.where(has_thr, jnp.int32(2 ** 30), -1))

        def mask_chunk(j, carry):
            whole = lanes16(j < cut_chunk)
            partial = lanes16(j == cut_chunk)
            for r in range(t // BF16_ROWS):
                hs = hi_scr[j, rows16(r), :]
                ls = lo_scr[j, rows16(r), :]
                in_cut = jnp.where(row16 + jnp.int16(BF16_ROWS * r) <= cut_row16, keep, drop)
                tie = jnp.where(whole, keep, jnp.where(partial, in_cut, drop))
                at_th = jnp.where(ls > tl16, keep, jnp.where(ls == tl16, tie, drop))
                mb = jnp.where(hs > th16, keep, jnp.where(hs == th16, at_th, drop))
                mask_scr[j, rows16(r), :] = mb.astype(F32)
            return carry

        lax.fori_loop(0, n_chunks, mask_chunk, 0)

    m_scr[...] = jnp.full(m_scr.shape, -jnp.inf, F32)
    acc_scr[...] = jnp.zeros(acc_scr.shape, F32)

    def attend(j, slot0, bias_sel, live=None):
        for h in range(N_HEADS):
            pair = slice(LANES * (h // 2), LANES * (h // 2) + LANES)
            kc = k_ref[0, pl.ds(pl.multiple_of(j * t, t), t), pair]
            mb = mask_scr[j]
            if live is not None:
                mb = jnp.where(live, mb, NEG_BIG)
            s = jnp.dot(kc, qpad_scr[h], preferred_element_type=F32) + mb
            if bias_sel is not None:
                s = s + bias_ref[bias_sel, h]
            s_scr[slot0 + h] = s
            m_old = m_scr[h:h + 1, :]
            m_new = jnp.maximum(m_old, jnp.max(s, axis=0, keepdims=True))
            alpha_scr[slot0 + h:slot0 + h + 1, :] = jnp.exp2(m_old - m_new)
            m_scr[h:h + 1, :] = m_new
        for h in range(N_HEADS):
            rows = slice(V_ROWS * h, V_ROWS * (h + 1))
            p = jnp.exp2((s_scr[slot0 + h] - m_scr[h:h + 1, :]).astype(BF16))
            pv = jnp.dot(vt_ref[0, j, rows, :], p, preferred_element_type=F32)
            acc_scr[rows, :] = alpha_scr[slot0 + h:slot0 + h + 1, :] * acc_scr[rows, :] + pv

    n_far = jnp.maximum(i - 1, 0)

    def far_pair(p, carry):
        attend(2 * p, 0, None)
        attend(2 * p + 1, N_HEADS, None)
        return carry

    lax.fori_loop(0, n_far // 2, far_pair, 0)

    @pl.when(n_far % 2 == 1)
    def _():
        attend(n_far - 1, 0, None)

    attend(jnp.maximum(i - 1, 0), 0, 1, live=i >= 1)
    attend(i, N_HEADS, 0)

    outs = []
    for h in range(N_HEADS):
        base = V_ROWS * h
        inv_l = 1.0 / acc_scr[base + HEAD_DIM:base + HEAD_DIM + 1, :]
        outs.append(acc_scr[base:base + HEAD_DIM, :] * inv_l)
    o_ref[0] = jnp.concatenate(outs, axis=0).T.astype(BF16)


def _attention(qt, qit, wit, k, vt4, ki2, bias, *, tile, topk):
    b, s, _ = k.shape
    nq = s // tile
    assert tile >= topk and tile >= _BIAS_REACH and s % tile == 0 and (tile & (tile - 1)) == 0
    assert nq % SCAN_GROUP == 0
    kern = functools.partial(_attn_kernel, tile=tile, topk=topk, tile_bits=tile.bit_length() - 1,
                             wi_scale=(N_IDX_HEADS ** -0.5) * (IDX_DIM ** -0.5))
    resident = pl.Buffered(1)
    return pl.pallas_call(
        kern,
        grid=(b, nq),
        in_specs=[
            pl.BlockSpec((1, ATTN_WIDTH, tile), lambda bb, i: (bb, 0, i)),
            pl.BlockSpec((1, ATTN_WIDTH, tile), lambda bb, i: (bb, 0, i)),
            pl.BlockSpec((1, N_IDX_HEADS, tile), lambda bb, i: (bb, 0, i)),
            pl.BlockSpec((1, s, ATTN_WIDTH), lambda bb, i: (bb, 0, 0), pipeline_mode=resident),
            pl.BlockSpec((1, nq, N_HEADS * V_ROWS, tile), lambda bb, i: (bb, 0, 0, 0),
                         pipeline_mode=resident),
            pl.BlockSpec((1, s, LANES), lambda bb, i: (bb, 0, 0), pipeline_mode=resident),
            pl.BlockSpec((2, N_HEADS, tile, tile), lambda bb, i: (0, 0, 0, 0),
                         pipeline_mode=resident),
        ],
        out_specs=pl.BlockSpec((1, tile, ATTN_WIDTH), lambda bb, i: (bb, i, 0)),
        out_shape=jax.ShapeDtypeStruct((b, s, ATTN_WIDTH), BF16),
        scratch_shapes=[
            pltpu.VMEM((nq, tile, tile), I16),
            pltpu.VMEM((nq, tile, tile), I16),
            pltpu.VMEM((nq, tile, tile), F32),
            pltpu.VMEM((2 * N_HEADS, tile, tile), F32),
            pltpu.VMEM((N_HEADS * V_ROWS, tile), F32),
            pltpu.VMEM((N_HEADS, tile), F32),
            pltpu.VMEM((2 * N_HEADS, tile), F32),
            pltpu.VMEM((nq, SUBLANES, tile), I32),
            pltpu.VMEM((tile, tile), I16),
            pltpu.VMEM((N_HEADS, LANES, tile), BF16),
            pltpu.VMEM((N_IDX_HEADS, LANES, tile), BF16),
        ],
        compiler_params=pltpu.CompilerParams(
            dimension_semantics=("arbitrary", "arbitrary"), vmem_limit_bytes=VMEM_LIMIT_BYTES),
        name="dsa_attention",
    )(qt, qit, wit, k, vt4, ki2, bias)


def _conv_kernel(u_ref, dw_ref, db_ref, g_ref, b_ref, o_ref, h_scr, *, tc, sub):
    ch = o_ref.shape[-1]
    halo = CONV_HALO

    @pl.when(pl.program_id(1) == 0)
    def _():
        h_scr[0:halo, :] = jnp.zeros((halo, ch), F32)

    u = u_ref[0]
    h_scr[halo:halo + tc, :] = u[:, :ch] * _sigmoid(u[:, ch:])
    first = halo - (CONV_WIDTH - 1)

    for r in range(tc // sub):
        base = r * sub
        acc = jnp.zeros((sub, ch), F32)
        for j in range(CONV_WIDTH):
            acc = acc + dw_ref[j:j + 1, :] * h_scr[base + first + j:base + first + j + sub, :]
        acc = acc + db_ref[...]
        mu = jnp.mean(acc, axis=-1, keepdims=True)
        cen = acc - mu
        var = jnp.mean(cen * cen, axis=-1, keepdims=True)
        y = cen * lax.rsqrt(var + EPS) * g_ref[...] + b_ref[...]
        o_ref[0, base:base + sub, :] = (y * _sigmoid(y)).astype(o_ref.dtype)
    h_scr[0:halo, :] = h_scr[tc:tc + halo, :]


def _conformer_conv(u, dw, db, g, bb, *, tc, sub=64):
    b, s, c2 = u.shape
    ch = c2 // 2
    const = lambda bi, i: (0, 0)
    return pl.pallas_call(
        functools.partial(_conv_kernel, tc=tc, sub=sub),
        grid=(b, s // tc),
        in_specs=[
            pl.BlockSpec((1, tc, c2), lambda bi, i: (bi, i, 0)),
            pl.BlockSpec(dw.shape, const),
            pl.BlockSpec((1, ch), const),
            pl.BlockSpec((1, ch), const),
            pl.BlockSpec((1, ch), const),
        ],
        out_specs=pl.BlockSpec((1, tc, ch), lambda bi, i: (bi, i, 0)),
        out_shape=jax.ShapeDtypeStruct((b, s, ch), BF16),
        scratch_shapes=[pltpu.VMEM((tc + CONV_HALO, ch), F32)],
        compiler_params=pltpu.CompilerParams(
            dimension_semantics=("arbitrary", "arbitrary"), vmem_limit_bytes=VMEM_LIMIT_BYTES),
        name="conformer_conv",
    )(u, dw, db, g, bb)


def _mix_kernel(x_ref, a_ref, c_ref, gg_ref, wa_ref, wc_ref, wm_ref, o_ref):
    d = x_ref.shape[-1]
    y_a = jnp.dot(a_ref[...], wa_ref[...], preferred_element_type=F32)
    y_b = jnp.dot(c_ref[...], wc_ref[...], preferred_element_type=F32)
    gg = gg_ref[...]
    merged = _sigmoid(gg[:, :d]) * y_a + _sigmoid(gg[:, d:]) * y_b
    o_ref[...] = x_ref[...] + jnp.dot(merged.astype(BF16), wm_ref[...], preferred_element_type=F32)


def _mix(x2d, attn, conv, gg, wa, wc, wm, *, tm):
    n, d = x2d.shape
    row = lambda i: (i, 0)
    const = lambda i: (0, 0)
    return pl.pallas_call(
        _mix_kernel,
        grid=(n // tm,),
        in_specs=[
            pl.BlockSpec((tm, d), row),
            pl.BlockSpec((tm, attn.shape[1]), row),
            pl.BlockSpec((tm, conv.shape[1]), row),
            pl.BlockSpec((tm, gg.shape[1]), row),
            pl.BlockSpec(wa.shape, const),
            pl.BlockSpec(wc.shape, const),
            pl.BlockSpec(wm.shape, const),
        ],
        out_specs=pl.BlockSpec((tm, d), row),
        out_shape=jax.ShapeDtypeStruct((n, d), F32),
        compiler_params=pltpu.CompilerParams(
            dimension_semantics=("parallel",), vmem_limit_bytes=VMEM_LIMIT_BYTES),
        name="gated_mix",
    )(x2d, attn, conv, gg, wa, wc, wm)


def _ffn_kernel(x_ref, g_ref, wg_ref, wu_ref, wo_ref, fg_ref, o_ref, h_scr, acc_scr, *, final_norm):
    j = pl.program_id(1)

    @pl.when(j == 0)
    def _():
        x = x_ref[...]
        ms = jnp.mean(x * x, axis=-1, keepdims=True)
        h_scr[...] = ((x * lax.rsqrt(ms + EPS)) * g_ref[...]).astype(BF16)
        acc_scr[...] = jnp.zeros(acc_scr.shape, F32)

    h = h_scr[...]
    gate = jnp.dot(h, wg_ref[...], preferred_element_type=F32)
    up = jnp.dot(h, wu_ref[...], preferred_element_type=F32)
    act = (gate * _sigmoid(gate) * up).astype(BF16)
    acc_scr[...] += jnp.dot(act, wo_ref[...], preferred_element_type=F32)

    @pl.when(j == pl.num_programs(1) - 1)
    def _():
        y = x_ref[...] + acc_scr[...]
        if final_norm:
            ms = jnp.mean(y * y, axis=-1, keepdims=True)
            y = (y * lax.rsqrt(ms + EPS)) * fg_ref[...]
        o_ref[...] = y


def _ffn(x2d, g, w_gate, w_up, w_out, fg, *, tm, tf, final_norm):
    n, d = x2d.shape
    dff = w_gate.shape[1]
    return pl.pallas_call(
        functools.partial(_ffn_kernel, final_norm=final_norm),
        grid=(n // tm, dff // tf),
        in_specs=[
            pl.BlockSpec((tm, d), lambda i, j: (i, 0)),
            pl.BlockSpec((1, d), lambda i, j: (0, 0)),
            pl.BlockSpec((d, tf), lambda i, j: (0, j)),
            pl.BlockSpec((d, tf), lambda i, j: (0, j)),
            pl.BlockSpec((tf, d), lambda i, j: (j, 0)),
            pl.BlockSpec((1, d), lambda i, j: (0, 0)),
        ],
        out_specs=pl.BlockSpec((tm, d), lambda i, j: (i, 0)),
        out_shape=jax.ShapeDtypeStruct((n, d), F32),
        scratch_shapes=[pltpu.VMEM((tm, d), BF16), pltpu.VMEM((tm, d), F32)],
        compiler_params=pltpu.CompilerParams(
            dimension_semantics=("parallel", "arbitrary"), vmem_limit_bytes=VMEM_LIMIT_BYTES),
        name="swiglu_ffn",
    )(x2d, g, w_gate, w_up, w_out, fg)


def _pick_tile(n, want):
    t = min(n, want)
    while n % t:
        t //= 2
    return t


def kernel(x, rel_bias, mix_norm, w_in, w_attn_out, dw_kernel, dw_bias, conv_norm_g, conv_norm_b,
           w_conv_out, w_mix_out, ffn_norm, w_ffn_in, w_ffn_out, final_norm):
    b, s, d = x.shape
    depth = w_in.shape[0]
    n = b * s
    conv_ch = d // 2
    d_ff = w_ffn_out.shape[1]
    topk = min(TOPK_MAX, s // 4)
    tile = ATTN_TILE
    nq = s // tile

    sizes = (ATTN_WIDTH, ATTN_WIDTH, ATTN_WIDTH, N_IDX_HEADS * IDX_DIM, IDX_DIM, N_IDX_HEADS,
             2 * conv_ch, d, d)
    offs = np.concatenate([[0], np.cumsum(sizes)])
    o_ki, o_wi, o_u, o_g = offs[4], offs[5], offs[6], offs[7]

    bias = _bias_tiles(rel_bias.astype(F32), tile)
    v_tail = jnp.zeros((b, nq, N_HEADS, V_ROWS - HEAD_DIM, tile), BF16).at[:, :, :, 0, :].set(1)

    x2d = x.reshape(n, d).astype(F32)
    for l in range(depth):
        w = w_in[l]
        wa = w[:, :o_ki].astype(BF16)
        w_ki = w[:, o_ki:o_wi]
        w_wi = w[:, o_wi:o_u]
        wkw = jnp.concatenate(
            [w_ki, w_ki, w_wi, jnp.zeros((d, LANES - N_IDX_HEADS), w.dtype)], axis=1).astype(BF16)
        wu = w[:, o_u:o_g].astype(BF16)
        wg = w[:, o_g:].astype(BF16)

        q, k, v, qi, ki2, wi, u, gg = _inproj(
            x2d, mix_norm[l].reshape(1, d).astype(F32), wa, wkw, wu, wg, tm=_pick_tile(n, 512))

        vt = v.reshape(b, nq, tile, N_HEADS, HEAD_DIM).transpose(0, 1, 3, 4, 2)
        vt4 = jnp.concatenate([vt, v_tail], axis=3).reshape(b, nq, N_HEADS * V_ROWS, tile)
        wit = wi[:, :N_IDX_HEADS].reshape(b, s, N_IDX_HEADS).transpose(0, 2, 1)
        qt = q.reshape(b, s, ATTN_WIDTH).transpose(0, 2, 1)
        qit = qi.reshape(b, s, ATTN_WIDTH).transpose(0, 2, 1)
        attn = _attention(qt, qit, wit, k.reshape(b, s, -1), vt4,
                          ki2.reshape(b, s, -1), bias, tile=tile, topk=topk)

        conv = _conformer_conv(
            u.reshape(b, s, -1), dw_kernel[l].astype(F32), dw_bias[l].reshape(1, -1).astype(F32),
            conv_norm_g[l].reshape(1, -1).astype(F32), conv_norm_b[l].reshape(1, -1).astype(F32),
            tc=_pick_tile(s, 512))

        x2d = _mix(x2d, attn.reshape(n, -1), conv.reshape(n, -1), gg,
                   w_attn_out[l].astype(BF16), w_conv_out[l].astype(BF16), w_mix_out[l].astype(BF16),
                   tm=_pick_tile(n, 512))

        wf = w_ffn_in[l]
        x2d = _ffn(x2d, ffn_norm[l].reshape(1, d).astype(F32),
                   wf[:, :d_ff].astype(BF16), wf[:, d_ff:].astype(BF16), w_ffn_out[l].astype(BF16),
                   final_norm.reshape(1, d).astype(F32),
                   tm=_pick_tile(n, 1024), tf=256, final_norm=(l == depth - 1))
    return x2d.reshape(b, s, d).astype(x.dtype)
```

```python
import functools
import math

import numpy as np
import jax
import jax.numpy as jnp
from jax import lax
from jax.experimental import pallas as pl
from jax.experimental.pallas import tpu as pltpu

F32 = jnp.float32
BF16 = jnp.bfloat16
I32 = jnp.int32
I16 = jnp.int16

N_HEADS = 8
HEAD_DIM = 64
ATTN_WIDTH = N_HEADS * HEAD_DIM
N_IDX_HEADS = 8
IDX_DIM = 64
TOPK_MAX = 256
CONV_WIDTH = 31
N_BUCKETS = 32
MAX_DISTANCE = 128
EPS = 1e-6

LANES = 128
SUBLANES = 8
BF16_ROWS = 16
VMEM_LIMIT_BYTES = 56 * 1024 * 1024
LOG2E = math.log2(math.e)
NEG_BIG = -1e30
INT_MIN = -(2 ** 31)
I16_MIN = -(2 ** 15)
I16_MAX = 2 ** 15 - 1

ATTN_TILE = 256
V_ROWS = HEAD_DIM + BF16_ROWS
SCAN_GROUP = 4
CONV_HALO = 32


def _sigmoid(x):
    return 1.0 / (1.0 + jnp.exp(-x))


def _t5_bucket_ranges():
    max_exact = N_BUCKETS // 2
    buckets = []
    for n in range(0, 4 * MAX_DISTANCE):
        if n < max_exact:
            buckets.append(n)
        else:
            b = max_exact + int(math.log(n / max_exact) / math.log(MAX_DISTANCE / max_exact)
                                * (N_BUCKETS - max_exact))
            buckets.append(min(b, N_BUCKETS - 1))
    ranges = []
    for b in range(N_BUCKETS - 1):
        ns = [n for n, bb in enumerate(buckets) if bb == b]
        ranges.append((min(ns), max(ns)))
    return ranges


_BUCKET_RANGES = _t5_bucket_ranges()
_BIAS_REACH = _BUCKET_RANGES[-1][1] + 1


def _inproj_kernel(x_ref, g_ref, wa_ref, wkw_ref, wu_ref, wg_ref,
                   q_ref, k_ref, v_ref, qi_ref, ki_ref, wi_ref, u_ref, gg_ref, *, q_scale):
    x = x_ref[...]
    ms = jnp.mean(x * x, axis=-1, keepdims=True)
    h = ((x * lax.rsqrt(ms + EPS)) * g_ref[...]).astype(BF16)

    def mm(w):
        return jnp.dot(h, w, preferred_element_type=F32)

    w = ATTN_WIDTH
    q_ref[...] = (mm(wa_ref[:, 0:w]) * q_scale).astype(BF16)
    k_ref[...] = mm(wa_ref[:, w:2 * w]).astype(BF16)
    v_ref[...] = mm(wa_ref[:, 2 * w:3 * w]).astype(BF16)
    qi_ref[...] = mm(wa_ref[:, 3 * w:4 * w]).astype(BF16)
    kw = mm(wkw_ref[...])
    ki_ref[...] = kw[:, :LANES].astype(BF16)
    wi_ref[...] = kw[:, LANES:]
    u_ref[...] = mm(wu_ref[...])
    gg_ref[...] = mm(wg_ref[...])


def _inproj(x2d, g, wa, wkw, wu, wg, *, tm):
    n, d = x2d.shape
    grid = (n // tm,)
    row = lambda i: (i, 0)
    const = lambda i: (0, 0)
    out_shape = (
        jax.ShapeDtypeStruct((n, ATTN_WIDTH), BF16),
        jax.ShapeDtypeStruct((n, ATTN_WIDTH), BF16),
        jax.ShapeDtypeStruct((n, ATTN_WIDTH), BF16),
        jax.ShapeDtypeStruct((n, ATTN_WIDTH), BF16),
        jax.ShapeDtypeStruct((n, LANES), BF16),
        jax.ShapeDtypeStruct((n, LANES), F32),
        jax.ShapeDtypeStruct((n, wu.shape[1]), F32),
        jax.ShapeDtypeStruct((n, wg.shape[1]), F32),
    )
    return pl.pallas_call(
        functools.partial(_inproj_kernel, q_scale=HEAD_DIM ** -0.5 * LOG2E),
        grid=grid,
        in_specs=[
            pl.BlockSpec((tm, d), row),
            pl.BlockSpec((1, d), const),
            pl.BlockSpec(wa.shape, const),
            pl.BlockSpec(wkw.shape, const),
            pl.BlockSpec(wu.shape, const),
            pl.BlockSpec(wg.shape, const),
        ],
        out_specs=[pl.BlockSpec((tm, s.shape[1]), row) for s in out_shape],
        out_shape=out_shape,
        compiler_params=pltpu.CompilerParams(
            dimension_semantics=("parallel",), vmem_limit_bytes=VMEM_LIMIT_BYTES),
        name="inproj",
    )(x2d, g, wa, wkw, wu, wg)


def _bias_kernel(tab_ref, out_ref, *, tile):
    r = lax.broadcasted_iota(I32, (tile, tile), 0)
    c = lax.broadcasted_iota(I32, (tile, tile), 1)
    for w in range(2):
        dist = c - r + w * tile
        for h in range(N_HEADS):
            far = tab_ref[N_BUCKETS - 1, h]
            val = jnp.zeros((tile, tile), F32)
            for b, (lo, hi) in enumerate(_BUCKET_RANGES):
                hit = (dist == lo) if lo == hi else ((dist >= lo) & (dist <= hi))
                val = jnp.where(hit, (tab_ref[b, h] - far) * LOG2E, val)
            out_ref[w, h] = val


def _bias_tiles(rel_bias, tile):
    return pl.pallas_call(
        functools.partial(_bias_kernel, tile=tile),
        in_specs=[pl.BlockSpec(memory_space=pltpu.SMEM)],
        out_specs=pl.BlockSpec(memory_space=pltpu.VMEM),
        out_shape=jax.ShapeDtypeStruct((2, N_HEADS, tile, tile), F32),
        name="bias_tiles",
    )(rel_bias)


def _order_key(x):
    return x ^ ((x >> 31) & 0x7FFFFFFF)


def _attn_kernel(qt_ref, qit_ref, wi_ref, k_ref, vt_ref, ki_ref, bias_ref, o_ref,
                 hi_scr, lo_scr, mask_scr, s_scr, acc_scr, m_scr, alpha_scr, tot_scr, col_scr,
                 qpad_scr, qipad_scr, *, tile, topk, tile_bits, wi_scale):
    t = tile
    i = pl.program_id(1)
    n_chunks = i + 1
    n_groups = n_chunks // SCAN_GROUP

    def fold(x, op):
        return op(x.reshape(t // SUBLANES, SUBLANES, t), axis=0)

    def rows16(r):
        return slice(BF16_ROWS * r, BF16_ROWS * (r + 1))

    def to16(x):
        return jnp.broadcast_to(x, (BF16_ROWS, t)).astype(I16)

    feat = lax.broadcasted_iota(I32, (LANES, t), 0)
    for h in range(N_HEADS):
        pair = slice(LANES * (h // 2), LANES * (h // 2) + LANES)
        lo = HEAD_DIM * (h % 2)
        in_head = (feat >= lo) & (feat < lo + HEAD_DIM)
        qpad_scr[h] = jnp.where(in_head, qt_ref[0, pair, :], 0).astype(BF16)
        qipad_scr[h] = jnp.where(in_head, qit_ref[0, pair, :], 0).astype(BF16)

    w_all = wi_ref[0] * wi_scale
    row = lax.broadcasted_iota(I32, (t, t), 0)
    col = lax.broadcasted_iota(I32, (t, t), 1)

    def score_chunk(j, slot, diagonal):
        kij = ki_ref[0, pl.ds(pl.multiple_of(j * t, t), t), :]
        for hp in range(N_IDX_HEADS // 2):
            ha, hb = 2 * hp, 2 * hp + 1
            la = jnp.dot(kij, qipad_scr[ha], preferred_element_type=F32)
            lb = jnp.dot(kij, qipad_scr[hb], preferred_element_type=F32)
            term = (jnp.maximum(la, 0.0) * w_all[ha:ha + 1, :]
                    + jnp.maximum(lb, 0.0) * w_all[hb:hb + 1, :])
            if hp == 0:
                s_scr[slot] = term
            elif hp < N_IDX_HEADS // 2 - 1:
                s_scr[slot] = s_scr[slot] + term
            else:
                score = s_scr[slot] + term
                score = jnp.where(score == 0.0, 0.0, score)
                skey = _order_key(pltpu.bitcast(score, I32))
                if diagonal:
                    skey = jnp.where(row <= col, skey, INT_MIN)
                hi_scr[j] = (skey >> 16).astype(I16)
                lo_scr[j] = ((skey & 0xFFFF) + I16_MIN).astype(I16)

    def score_pair(p, carry):
        score_chunk(2 * p, 0, False)
        score_chunk(2 * p + 1, 1, False)
        return carry

    lax.fori_loop(0, i // 2, score_pair, 0)

    @pl.when(i % 2 == 1)
    def _():
        score_chunk(i - 1, 0, False)

    score_chunk(i, 1, True)

    def count_ge16(plane, cands):
        cands16 = [to16(c) for c in cands]

        def scan_chunk(j, parts):
            parts = list(parts)
            for ci, c16 in enumerate(cands16):
                acc = parts[ci]
                for r in range(t // BF16_ROWS):
                    acc = acc + jnp.where(plane[j, rows16(r), :] >= c16, jnp.int16(1), jnp.int16(0))
                parts[ci] = acc
            return tuple(parts)

        def scan_group(g, parts):
            for u in range(SCAN_GROUP):
                parts = scan_chunk(g * SCAN_GROUP + u, parts)
            return parts

        zero = jnp.zeros((BF16_ROWS, t), I16)
        parts = lax.fori_loop(0, n_groups, scan_group, tuple(zero for _ in cands))
        parts = lax.fori_loop(n_groups * SCAN_GROUP, n_chunks, scan_chunk, parts)
        return [jnp.sum(p.astype(I32), axis=0, keepdims=True) for p in parts]

    def select16(plane, want):
        def bit_step(it, state):
            cur_u, c_ge, c_gt = state
            cand_u = cur_u | (jnp.int32(1) << (15 - it))
            cnt, = count_ge16(plane, [cand_u + I16_MIN])
            ok = cnt >= want
            return jnp.where(ok, cand_u, cur_u), jnp.where(ok, cnt, c_ge), jnp.where(ok, c_gt, cnt)

        everything = jnp.full((1, t), n_chunks * t, I32)
        cur_u, c_ge, c_gt = lax.fori_loop(
            0, 16, bit_step, (jnp.zeros((1, t), I32), everything, jnp.zeros((1, t), I32)))
        return cur_u + I16_MIN, c_ge, c_gt

    th, c_hi_ge, c_above = select16(hi_scr, jnp.full((1, t), topk, I32))
    th16 = to16(th)

    def low_candidates(j, carry):
        for r in range(t // BF16_ROWS):
            lo_scr[j, rows16(r), :] = jnp.where(hi_scr[j, rows16(r), :] == th16,
                                                lo_scr[j, rows16(r), :], jnp.int16(I16_MIN))
        return carry

    lax.fori_loop(0, n_chunks, low_candidates, 0)

    tl, c_lo_ge, c_lo_gt = select16(lo_scr, topk - c_above)
    c_ge = jnp.where(tl == I16_MIN, c_hi_ge, c_above + c_lo_ge)
    need = topk - (c_above + c_lo_gt)
    has_thr = th > I16_MIN
    tie_split = (c_ge > topk) & has_thr
    any_split = jnp.max(jnp.where(tie_split, 1, 0)) > 0

    tl16 = to16(tl)
    keep = jnp.zeros((BF16_ROWS, t), BF16)
    drop = jnp.full((BF16_ROWS, t), NEG_BIG, BF16)
    one16 = jnp.ones((BF16_ROWS, t), I16)
    zero16 = jnp.zeros((BF16_ROWS, t), I16)

    def lanes16(pred):
        return to16(jnp.where(pred, 1, 0)) > 0

    @pl.when(jnp.logical_not(any_split))
    def _():
        tie = jnp.where(lanes16(has_thr), keep, drop)

        def mask_chunk(j, carry):
            for r in range(t // BF16_ROWS):
                hs = hi_scr[j, rows16(r), :]
                ls = lo_scr[j, rows16(r), :]
                at_th = jnp.where(ls > tl16, keep, jnp.where(ls == tl16, tie, drop))
                mb = jnp.where(hs > th16, keep, jnp.where(hs == th16, at_th, drop))
                mask_scr[j, rows16(r), :] = mb.astype(F32)
            return carry

        lax.fori_loop(0, n_chunks, mask_chunk, 0)

    @pl.when(any_split)
    def _():
        row16 = lax.broadcasted_iota(I32, (BF16_ROWS, t), 0).astype(I16)

        def tie_tile(j, r):
            hs = hi_scr[j, rows16(r), :]
            ls = lo_scr[j, rows16(r), :]
            return jnp.where(hs == th16, jnp.where(ls == tl16, one16, zero16), zero16)

        def chunk_total(j, carry):
            acc = zero16
            for r in range(t // BF16_ROWS):
                acc = acc + tie_tile(j, r)
            tot = jnp.sum(acc.astype(I32), axis=0, keepdims=True)
            tot_scr[j] = jnp.broadcast_to(tot, (SUBLANES, t))
            return carry

        lax.fori_loop(0, n_chunks, chunk_total, 0)

        def locate_chunk(j, carry):
            cum, n_before, ties_before = carry
            tot = tot_scr[j][0:1, :]
            cum = cum + tot
            before = cum < need
            return cum, n_before + jnp.where(before, 1, 0), ties_before + jnp.where(before, tot, 0)

        zero1 = jnp.zeros((1, t), I32)
        _, cut_chunk, ties_before = lax.fori_loop(0, n_chunks, locate_chunk, (zero1, zero1, zero1))

        col_scr[...] = jnp.zeros(col_scr.shape, I16)

        def gather_chunk(j, carry):
            here = lanes16(cut_chunk == j)
            for r in range(t // BF16_ROWS):
                col_scr[rows16(r), :] = jnp.where(here, tie_tile(j, r), col_scr[rows16(r), :])
            return carry

        lax.fori_loop(0, n_chunks, gather_chunk, 0)

        wanted = need - ties_before

        def row_bit(it, cur):
            cand = cur | (jnp.int32(1) << (tile_bits - 1 - it))
            cand16 = to16(cand)
            acc = zero16
            for r in range(t // BF16_ROWS):
                acc = acc + jnp.where(row16 + jnp.int16(BF16_ROWS * r) < cand16,
                                      col_scr[rows16(r), :], zero16)
            cnt = jnp.sum(acc.astype(I32), axis=0, keepdims=True)
            return jnp.where(cnt < wanted, cand, cur)

        cut_row16 = to16(lax.fori_loop(0, tile_bits, row_bit, zero1))
        cut_chunk = jnp.where(tie_split, cut_chunk, jnp.where(has_thr, jnp.int32(2 ** 30), -1))

        def mask_chunk(j, carry):
            whole = lanes16(j < cut_chunk)
            partial = lanes16(j == cut_chunk)
            for r in range(t // BF16_ROWS):
                hs = hi_scr[j, rows16(r), :]
                ls = lo_scr[j, rows16(r), :]
                in_cut = jnp.where(row16 + jnp.int16(BF16_ROWS * r) <= cut_row16, keep, drop)
                tie = jnp.where(whole, keep, jnp.where(partial, in_cut, drop))
                at_th = jnp.where(ls > tl16, keep, jnp.where(ls == tl16, tie, drop))
                mb = jnp.where(hs > th16, keep, jnp.where(hs == th16, at_th, drop))
                mask_scr[j, rows16(r), :] = mb.astype(F32)
            return carry

        lax.fori_loop(0, n_chunks, mask_chunk, 0)

    m_scr[...] = jnp.full(m_scr.shape, -jnp.inf, F32)
    acc_scr[...] = jnp.zeros(acc_scr.shape, F32)

    def attend(j, slot0, bias_sel, live=None):
        for h in range(N_HEADS):
            pair = slice(LANES * (h // 2), LANES * (h // 2) + LANES)
            kc = k_ref[0, pl.ds(pl.multiple_of(j * t, t), t), pair]
            mb = mask_scr[j]
            if live is not None:
                mb = jnp.where(live, mb, NEG_BIG)
            s = jnp.dot(kc, qpad_scr[h], preferred_element_type=F32) + mb
            if bias_sel is not None:
                s = s + bias_ref[bias_sel, h]
            s_scr[slot0 + h] = s
            m_old = m_scr[h:h + 1, :]
            m_new = jnp.maximum(m_old, jnp.max(s, axis=0, keepdims=True))
            alpha_scr[slot0 + h:slot0 + h + 1, :] = jnp.exp2(m_old - m_new)
            m_scr[h:h + 1, :] = m_new
        for h in range(N_HEADS):
            rows = slice(V_ROWS * h, V_ROWS * (h + 1))
            p = jnp.exp2((s_scr[slot0 + h] - m_scr[h:h + 1, :]).astype(BF16))
            pv = jnp.dot(vt_ref[0, j, rows, :], p, preferred_element_type=F32)
            acc_scr[rows, :] = alpha_scr[slot0 + h:slot0 + h + 1, :] * acc_scr[rows, :] + pv

    n_far = jnp.maximum(i - 1, 0)

    def far_pair(p, carry):
        attend(2 * p, 0, None)
        attend(2 * p + 1, N_HEADS, None)
        return carry

    lax.fori_loop(0, n_far // 2, far_pair, 0)

    @pl.when(n_far % 2 == 1)
    def _():
        attend(n_far - 1, 0, None)

    attend(jnp.maximum(i - 1, 0), 0, 1, live=i >= 1)
    attend(i, N_HEADS, 0)

    outs = []
    for h in range(N_HEADS):
        base = V_ROWS * h
        inv_l = 1.0 / acc_scr[base + HEAD_DIM:base + HEAD_DIM + 1, :]
        outs.append(acc_scr[base:base + HEAD_DIM, :] * inv_l)
    o_ref[0] = jnp.concatenate(outs, axis=0).T.astype(BF16)


def _attention(qt, qit, wit, k, vt4, ki2, bias, *, tile, topk):
    b, s, _ = k.shape
    nq = s // tile
    assert tile >= topk and tile >= _BIAS_REACH and s % tile == 0 and (tile & (tile - 1)) == 0
    kern = functools.partial(_attn_kernel, tile=tile, topk=topk, tile_bits=tile.bit_length() - 1,
                             wi_scale=(N_IDX_HEADS ** -0.5) * (IDX_DIM ** -0.5))
    resident = pl.Buffered(1)
    return pl.pallas_call(
        kern,
        grid=(b, nq),
        in_specs=[
            pl.BlockSpec((1, ATTN_WIDTH, tile), lambda bb, i: (bb, 0, i)),
            pl.BlockSpec((1, ATTN_WIDTH, tile), lambda bb, i: (bb, 0, i)),
            pl.BlockSpec((1, N_IDX_HEADS, tile), lambda bb, i: (bb, 0, i)),
            pl.BlockSpec((1, s, ATTN_WIDTH), lambda bb, i: (bb, 0, 0), pipeline_mode=resident),
            pl.BlockSpec((1, nq, N_HEADS * V_ROWS, tile), lambda bb, i: (bb, 0, 0, 0),
                         pipeline_mode=resident),
            pl.BlockSpec((1, s, LANES), lambda bb, i: (bb, 0, 0), pipeline_mode=resident),
            pl.BlockSpec((2, N_HEADS, tile, tile), lambda bb, i: (0, 0, 0, 0),
                         pipeline_mode=resident),
        ],
        out_specs=pl.BlockSpec((1, tile, ATTN_WIDTH), lambda bb, i: (bb, i, 0)),
        out_shape=jax.ShapeDtypeStruct((b, s, ATTN_WIDTH), BF16),
        scratch_shapes=[
            pltpu.VMEM((nq, tile, tile), I16),
            pltpu.VMEM((nq, tile, tile), I16),
            pltpu.VMEM((nq, tile, tile), F32),
            pltpu.VMEM((2 * N_HEADS, tile, tile), F32),
            pltpu.VMEM((N_HEADS * V_ROWS, tile), F32),
            pltpu.VMEM((N_HEADS, tile), F32),
            pltpu.VMEM((2 * N_HEADS, tile), F32),
            pltpu.VMEM((nq, SUBLANES, tile), I32),
            pltpu.VMEM((tile, tile), I16),
            pltpu.VMEM((N_HEADS, LANES, tile), BF16),
            pltpu.VMEM((N_IDX_HEADS, LANES, tile), BF16),
        ],
        compiler_params=pltpu.CompilerParams(
            dimension_semantics=("arbitrary", "arbitrary"), vmem_limit_bytes=VMEM_LIMIT_BYTES),
        name="dsa_attention",
    )(qt, qit, wit, k, vt4, ki2, bias)


def _conv_kernel(u_ref, dw_ref, db_ref, g_ref, b_ref, o_ref, h_scr, shift_scr, *, tc, sub):
    ch = o_ref.shape[-1]
    halo = CONV_HALO

    @pl.when(pl.program_id(1) == 0)
    def _():
        h_scr[0:halo, :] = jnp.zeros((halo, ch), F32)

    u = u_ref[0]
    h_scr[halo:halo + tc, :] = u[:, :ch] * _sigmoid(u[:, ch:])
    first = halo - (CONV_WIDTH - 1)

    span = tc + halo - SUBLANES
    for s in range(1, SUBLANES):
        shift_scr[s - 1, 0:span, :] = h_scr[s:s + span, :]

    for r in range(tc // sub):
        base = r * sub
        acc = jnp.zeros((sub, ch), F32)
        for j in range(CONV_WIDTH):
            s = (first + j) % SUBLANES
            lo = base + first + j - s
            rows = h_scr[lo:lo + sub, :] if s == 0 else shift_scr[s - 1, lo:lo + sub, :]
            acc = acc + dw_ref[j:j + 1, :] * rows
        acc = acc + db_ref[...]
        mu = jnp.mean(acc, axis=-1, keepdims=True)
        cen = acc - mu
        var = jnp.mean(cen * cen, axis=-1, keepdims=True)
        y = cen * lax.rsqrt(var + EPS) * g_ref[...] + b_ref[...]
        o_ref[0, base:base + sub, :] = (y * _sigmoid(y)).astype(o_ref.dtype)
    h_scr[0:halo, :] = h_scr[tc:tc + halo, :]


def _conformer_conv(u, dw, db, g, bb, *, tc, sub=64):
    b, s, c2 = u.shape
    ch = c2 // 2
    const = lambda bi, i: (0, 0)
    return pl.pallas_call(
        functools.partial(_conv_kernel, tc=tc, sub=sub),
        grid=(b, s // tc),
        in_specs=[
            pl.BlockSpec((1, tc, c2), lambda bi, i: (bi, i, 0)),
            pl.BlockSpec(dw.shape, const),
            pl.BlockSpec((1, ch), const),
            pl.BlockSpec((1, ch), const),
            pl.BlockSpec((1, ch), const),
        ],
        out_specs=pl.BlockSpec((1, tc, ch), lambda bi, i: (bi, i, 0)),
        out_shape=jax.ShapeDtypeStruct((b, s, ch), BF16),
        scratch_shapes=[pltpu.VMEM((tc + CONV_HALO, ch), F32),
                        pltpu.VMEM((SUBLANES - 1, tc + CONV_HALO, ch), F32)],
        compiler_params=pltpu.CompilerParams(
            dimension_semantics=("arbitrary", "arbitrary"), vmem_limit_bytes=VMEM_LIMIT_BYTES),
        name="conformer_conv",
    )(u, dw, db, g, bb)


def _mix_kernel(x_ref, a_ref, c_ref, gg_ref, wa_ref, wc_ref, wm_ref, o_ref):
    d = x_ref.shape[-1]
    y_a = jnp.dot(a_ref[...], wa_ref[...], preferred_element_type=F32)
    y_b = jnp.dot(c_ref[...], wc_ref[...], preferred_element_type=F32)
    gg = gg_ref[...]
    merged = _sigmoid(gg[:, :d]) * y_a + _sigmoid(gg[:, d:]) * y_b
    o_ref[...] = x_ref[...] + jnp.dot(merged.astype(BF16), wm_ref[...], preferred_element_type=F32)


def _mix(x2d, attn, conv, gg, wa, wc, wm, *, tm):
    n, d = x2d.shape
    row = lambda i: (i, 0)
    const = lambda i: (0, 0)
    return pl.pallas_call(
        _mix_kernel,
        grid=(n // tm,),
        in_specs=[
            pl.BlockSpec((tm, d), row),
            pl.BlockSpec((tm, attn.shape[1]), row),
            pl.BlockSpec((tm, conv.shape[1]), row),
            pl.BlockSpec((tm, gg.shape[1]), row),
            pl.BlockSpec(wa.shape, const),
            pl.BlockSpec(wc.shape, const),
            pl.BlockSpec(wm.shape, const),
        ],
        out_specs=pl.BlockSpec((tm, d), row),
        out_shape=jax.ShapeDtypeStruct((n, d), F32),
        compiler_params=pltpu.CompilerParams(
            dimension_semantics=("parallel",), vmem_limit_bytes=VMEM_LIMIT_BYTES),
        name="gated_mix",
    )(x2d, attn, conv, gg, wa, wc, wm)


def _ffn_kernel(x_ref, g_ref, wg_ref, wu_ref, wo_ref, fg_ref, o_ref, h_scr, acc_scr, *, final_norm):
    j = pl.program_id(1)

    @pl.when(j == 0)
    def _():
        x = x_ref[...]
        ms = jnp.mean(x * x, axis=-1, keepdims=True)
        h_scr[...] = ((x * lax.rsqrt(ms + EPS)) * g_ref[...]).astype(BF16)
        acc_scr[...] = jnp.zeros(acc_scr.shape, F32)

    h = h_scr[...]
    gate = jnp.dot(h, wg_ref[...], preferred_element_type=F32)
    up = jnp.dot(h, wu_ref[...], preferred_element_type=F32)
    act = (gate * _sigmoid(gate) * up).astype(BF16)
    acc_scr[...] += jnp.dot(act, wo_ref[...], preferred_element_type=F32)

    @pl.when(j == pl.num_programs(1) - 1)
    def _():
        y = x_ref[...] + acc_scr[...]
        if final_norm:
            ms = jnp.mean(y * y, axis=-1, keepdims=True)
            y = (y * lax.rsqrt(ms + EPS)) * fg_ref[...]
        o_ref[...] = y


def _ffn(x2d, g, w_gate, w_up, w_out, fg, *, tm, tf, final_norm):
    n, d = x2d.shape
    dff = w_gate.shape[1]
    return pl.pallas_call(
        functools.partial(_ffn_kernel, final_norm=final_norm),
        grid=(n // tm, dff // tf),
        in_specs=[
            pl.BlockSpec((tm, d), lambda i, j: (i, 0)),
            pl.BlockSpec((1, d), lambda i, j: (0, 0)),
            pl.BlockSpec((d, tf), lambda i, j: (0, j)),
            pl.BlockSpec((d, tf), lambda i, j: (0, j)),
            pl.BlockSpec((tf, d), lambda i, j: (j, 0)),
            pl.BlockSpec((1, d), lambda i, j: (0, 0)),
        ],
        out_specs=pl.BlockSpec((tm, d), lambda i, j: (i, 0)),
        out_shape=jax.ShapeDtypeStruct((n, d), F32),
        scratch_shapes=[pltpu.VMEM((tm, d), BF16), pltpu.VMEM((tm, d), F32)],
        compiler_params=pltpu.CompilerParams(
            dimension_semantics=("parallel", "arbitrary"), vmem_limit_bytes=VMEM_LIMIT_BYTES),
        name="swiglu_ffn",
    )(x2d, g, w_gate, w_up, w_out, fg)


def _pick_tile(n, want):
    t = min(n, want)
    while n % t:
        t //= 2
    return t


def kernel(x, rel_bias, mix_norm, w_in, w_attn_out, dw_kernel, dw_bias, conv_norm_g, conv_norm_b,
           w_conv_out, w_mix_out, ffn_norm, w_ffn_in, w_ffn_out, final_norm):
    b, s, d = x.shape
    depth = w_in.shape[0]
    n = b * s
    conv_ch = d // 2
    d_ff = w_ffn_out.shape[1]
    topk = min(TOPK_MAX, s // 4)
    tile = ATTN_TILE
    nq = s // tile

    sizes = (ATTN_WIDTH, ATTN_WIDTH, ATTN_WIDTH, N_IDX_HEADS * IDX_DIM, IDX_DIM, N_IDX_HEADS,
             2 * conv_ch, d, d)
    offs = np.concatenate([[0], np.cumsum(sizes)])
    o_ki, o_wi, o_u, o_g = offs[4], offs[5], offs[6], offs[7]

    bias = _bias_tiles(rel_bias.astype(F32), tile)
    v_tail = jnp.zeros((b, nq, N_HEADS, V_ROWS - HEAD_DIM, tile), BF16).at[:, :, :, 0, :].set(1)

    x2d = x.reshape(n, d).astype(F32)
    for l in range(depth):
        w = w_in[l]
        wa = w[:, :o_ki].astype(BF16)
        w_ki = w[:, o_ki:o_wi]
        w_wi = w[:, o_wi:o_u]
        wkw = jnp.concatenate(
            [w_ki, w_ki, w_wi, jnp.zeros((d, LANES - N_IDX_HEADS), w.dtype)], axis=1).astype(BF16)
        wu = w[:, o_u:o_g].astype(BF16)
        wg = w[:, o_g:].astype(BF16)

        q, k, v, qi, ki2, wi, u, gg = _inproj(
            x2d, mix_norm[l].reshape(1, d).astype(F32), wa, wkw, wu, wg, tm=_pick_tile(n, 512))

        vt = v.reshape(b, nq, tile, N_HEADS, HEAD_DIM).transpose(0, 1, 3, 4, 2)
        vt4 = jnp.concatenate([vt, v_tail], axis=3).reshape(b, nq, N_HEADS * V_ROWS, tile)
        wit = wi[:, :N_IDX_HEADS].reshape(b, s, N_IDX_HEADS).transpose(0, 2, 1)
        qt = q.reshape(b, s, ATTN_WIDTH).transpose(0, 2, 1)
        qit = qi.reshape(b, s, ATTN_WIDTH).transpose(0, 2, 1)
        attn = _attention(qt, qit, wit, k.reshape(b, s, -1), vt4,
                          ki2.reshape(b, s, -1), bias, tile=tile, topk=topk)

        conv = _conformer_conv(
            u.reshape(b, s, -1), dw_kernel[l].astype(F32), dw_bias[l].reshape(1, -1).astype(F32),
            conv_norm_g[l].reshape(1, -1).astype(F32), conv_norm_b[l].reshape(1, -1).astype(F32),
            tc=_pick_tile(s, 512))

        x2d = _mix(x2d, attn.reshape(n, -1), conv.reshape(n, -1), gg,
                   w_attn_out[l].astype(BF16), w_conv_out[l].astype(BF16), w_mix_out[l].astype(BF16),
                   tm=_pick_tile(n, 512))

        wf = w_ffn_in[l]
        x2d = _ffn(x2d, ffn_norm[l].reshape(1, d).astype(F32),
                   wf[:, :d_ff].astype(BF16), wf[:, d_ff:].astype(BF16), w_ffn_out[l].astype(BF16),
                   final_norm.reshape(1, d).astype(F32),
                   tm=_pick_tile(n, 1024), tf=256, final_norm=(l == depth - 1))
    return x2d.reshape(b, s, d).astype(x.dtype)
```

```python
import functools
import math

import numpy as np
import jax
import jax.numpy as jnp
from jax import lax
from jax.experimental import pallas as pl
from jax.experimental.pallas import tpu as pltpu

F32 = jnp.float32
BF16 = jnp.bfloat16
I32 = jnp.int32
I16 = jnp.int16

N_HEADS = 8
HEAD_DIM = 64
ATTN_WIDTH = N_HEADS * HEAD_DIM
N_IDX_HEADS = 8
IDX_DIM = 64
TOPK_MAX = 256
CONV_WIDTH = 31
N_BUCKETS = 32
MAX_DISTANCE = 128
EPS = 1e-6

LANES = 128
SUBLANES = 8
BF16_ROWS = 16
VMEM_LIMIT_BYTES = 56 * 1024 * 1024
LOG2E = math.log2(math.e)
NEG_BIG = -1e30
INT_MIN = -(2 ** 31)
I16_MIN = -(2 ** 15)
I16_MAX = 2 ** 15 - 1

ATTN_TILE = 256
V_ROWS = HEAD_DIM + BF16_ROWS
SCAN_GROUP = 4
CONV_HALO = 32


def _sigmoid(x):
    return 1.0 / (1.0 + jnp.exp(-x))


def _t5_bucket_ranges():
    max_exact = N_BUCKETS // 2
    buckets = []
    for n in range(0, 4 * MAX_DISTANCE):
        if n < max_exact:
            buckets.append(n)
        else:
            b = max_exact + int(math.log(n / max_exact) / math.log(MAX_DISTANCE / max_exact)
                                * (N_BUCKETS - max_exact))
            buckets.append(min(b, N_BUCKETS - 1))
    ranges = []
    for b in range(N_BUCKETS - 1):
        ns = [n for n, bb in enumerate(buckets) if bb == b]
        ranges.append((min(ns), max(ns)))
    return ranges


_BUCKET_RANGES = _t5_bucket_ranges()
_BIAS_REACH = _BUCKET_RANGES[-1][1] + 1


def _inproj_kernel(x_ref, g_ref, wa_ref, wkw_ref, wu_ref, wg_ref,
                   qt_ref, k_ref, vt_ref, qit_ref, ki_ref, wi_ref, u_ref, gg_ref, t_scr,
                   *, q_scale, tile):
    x = x_ref[...]
    tm = x.shape[0]
    ms = jnp.mean(x * x, axis=-1, keepdims=True)
    h = ((x * lax.rsqrt(ms + EPS)) * g_ref[...]).astype(BF16)

    def mm(w):
        return jnp.dot(h, w, preferred_element_type=F32)

    w = ATTN_WIDTH

    def transposed(y):
        t_scr[...] = y
        return t_scr[...].T.astype(BF16)

    qt_ref[0] = transposed(mm(wa_ref[:, 0:w]) * q_scale)
    k_ref[...] = mm(wa_ref[:, w:2 * w]).astype(BF16)
    qit_ref[0] = transposed(mm(wa_ref[:, 3 * w:4 * w]))
    vt = transposed(mm(wa_ref[:, 2 * w:3 * w]))
    tail_row = lax.broadcasted_iota(I32, (V_ROWS - HEAD_DIM, tile), 0)
    tail = jnp.where(tail_row == 0, 1.0, 0.0).astype(BF16)
    for c in range(tm // tile):
        for hd in range(N_HEADS):
            base = V_ROWS * hd
            vt_ref[0, c, base:base + HEAD_DIM, :] = vt[HEAD_DIM * hd:HEAD_DIM * (hd + 1),
                                                       tile * c:tile * (c + 1)]
            vt_ref[0, c, base + HEAD_DIM:base + V_ROWS, :] = tail
    kw = mm(wkw_ref[...])
    ki_ref[...] = kw[:, :LANES].astype(BF16)
    wi_ref[...] = kw[:, LANES:]
    u_ref[...] = mm(wu_ref[...])
    gg_ref[...] = mm(wg_ref[...])


def _inproj(x2d, g, wa, wkw, wu, wg, *, batch, tm, tile):
    n, d = x2d.shape
    seq = n // batch
    assert seq % tm == 0 and tm % tile == 0
    spb = seq // tm
    grid = (n // tm,)
    row = lambda i: (i, 0)
    const = lambda i: (0, 0)
    feat_major = lambda i: (i // spb, 0, i % spb)
    out_shape = (
        jax.ShapeDtypeStruct((batch, ATTN_WIDTH, seq), BF16),
        jax.ShapeDtypeStruct((n, ATTN_WIDTH), BF16),
        jax.ShapeDtypeStruct((batch, seq // tile, N_HEADS * V_ROWS, tile), BF16),
        jax.ShapeDtypeStruct((batch, ATTN_WIDTH, seq), BF16),
        jax.ShapeDtypeStruct((n, LANES), BF16),
        jax.ShapeDtypeStruct((n, LANES), F32),
        jax.ShapeDtypeStruct((n, wu.shape[1]), F32),
        jax.ShapeDtypeStruct((n, wg.shape[1]), F32),
    )
    out_specs = [
        pl.BlockSpec((1, ATTN_WIDTH, tm), feat_major),
        pl.BlockSpec((tm, ATTN_WIDTH), row),
        pl.BlockSpec((1, tm // tile, N_HEADS * V_ROWS, tile), lambda i: (i // spb, i % spb, 0, 0)),
        pl.BlockSpec((1, ATTN_WIDTH, tm), feat_major),
        pl.BlockSpec((tm, LANES), row),
        pl.BlockSpec((tm, LANES), row),
        pl.BlockSpec((tm, wu.shape[1]), row),
        pl.BlockSpec((tm, wg.shape[1]), row),
    ]
    return pl.pallas_call(
        functools.partial(_inproj_kernel, q_scale=HEAD_DIM ** -0.5 * LOG2E, tile=tile),
        grid=grid,
        in_specs=[
            pl.BlockSpec((tm, d), row),
            pl.BlockSpec((1, d), const),
            pl.BlockSpec(wa.shape, const),
            pl.BlockSpec(wkw.shape, const),
            pl.BlockSpec(wu.shape, const),
            pl.BlockSpec(wg.shape, const),
        ],
        out_specs=out_specs,
        out_shape=out_shape,
        scratch_shapes=[pltpu.VMEM((tm, ATTN_WIDTH), F32)],
        compiler_params=pltpu.CompilerParams(
            dimension_semantics=("parallel",), vmem_limit_bytes=VMEM_LIMIT_BYTES),
        name="inproj",
    )(x2d, g, wa, wkw, wu, wg)


def _bias_kernel(tab_ref, out_ref, *, tile):
    r = lax.broadcasted_iota(I32, (tile, tile), 0)
    c = lax.broadcasted_iota(I32, (tile, tile), 1)
    for w in range(2):
        dist = c - r + w * tile
        for h in range(N_HEADS):
            far = tab_ref[N_BUCKETS - 1, h]
            val = jnp.zeros((tile, tile), F32)
            for b, (lo, hi) in enumerate(_BUCKET_RANGES):
                hit = (dist == lo) if lo == hi else ((dist >= lo) & (dist <= hi))
                val = jnp.where(hit, (tab_ref[b, h] - far) * LOG2E, val)
            out_ref[w, h] = val


def _bias_tiles(rel_bias, tile):
    return pl.pallas_call(
        functools.partial(_bias_kernel, tile=tile),
        in_specs=[pl.BlockSpec(memory_space=pltpu.SMEM)],
        out_specs=pl.BlockSpec(memory_space=pltpu.VMEM),
        out_shape=jax.ShapeDtypeStruct((2, N_HEADS, tile, tile), F32),
        name="bias_tiles",
    )(rel_bias)


def _order_key(x):
    return x ^ ((x >> 31) & 0x7FFFFFFF)


def _attn_kernel(qt_ref, qit_ref, wi_ref, k_ref, vt_ref, ki_ref, bias_ref, o_ref,
                 hi_scr, lo_scr, mask_scr, s_scr, acc_scr, m_scr, alpha_scr, tot_scr, col_scr,
                 qpad_scr, qipad_scr, *, tile, topk, tile_bits, wi_scale):
    t = tile
    i = pl.program_id(1)
    n_chunks = i + 1
    n_groups = n_chunks // SCAN_GROUP

    def fold(x, op):
        return op(x.reshape(t // SUBLANES, SUBLANES, t), axis=0)

    def rows16(r):
        return slice(BF16_ROWS * r, BF16_ROWS * (r + 1))

    def to16(x):
        return jnp.broadcast_to(x, (BF16_ROWS, t)).astype(I16)

    feat = lax.broadcasted_iota(I32, (LANES, t), 0)
    for h in range(N_HEADS):
        pair = slice(LANES * (h // 2), LANES * (h // 2) + LANES)
        lo = HEAD_DIM * (h % 2)
        in_head = (feat >= lo) & (feat < lo + HEAD_DIM)
        qpad_scr[h] = jnp.where(in_head, qt_ref[0, pair, :], 0).astype(BF16)
        qipad_scr[h] = jnp.where(in_head, qit_ref[0, pair, :], 0).astype(BF16)

    w_all = wi_ref[0] * wi_scale
    row = lax.broadcasted_iota(I32, (t, t), 0)
    col = lax.broadcasted_iota(I32, (t, t), 1)

    def score_chunk(j, slot, diagonal):
        kij = ki_ref[0, pl.ds(pl.multiple_of(j * t, t), t), :]
        for hp in range(N_IDX_HEADS // 2):
            ha, hb = 2 * hp, 2 * hp + 1
            la = jnp.dot(kij, qipad_scr[ha], preferred_element_type=F32)
            lb = jnp.dot(kij, qipad_scr[hb], preferred_element_type=F32)
            term = (jnp.maximum(la, 0.0) * w_all[ha:ha + 1, :]
                    + jnp.maximum(lb, 0.0) * w_all[hb:hb + 1, :])
            if hp == 0:
                s_scr[slot] = term
            elif hp < N_IDX_HEADS // 2 - 1:
                s_scr[slot] = s_scr[slot] + term
            else:
                score = s_scr[slot] + term
                score = jnp.where(score == 0.0, 0.0, score)
                skey = _order_key(pltpu.bitcast(score, I32))
                if diagonal:
                    skey = jnp.where(row <= col, skey, INT_MIN)
                hi_scr[j] = (skey >> 16).astype(I16)
                lo_scr[j] = ((skey & 0xFFFF) + I16_MIN).astype(I16)

    def score_pair(p, carry):
        score_chunk(2 * p, 0, False)
        score_chunk(2 * p + 1, 1, False)
        return carry

    lax.fori_loop(0, i // 2, score_pair, 0)

    @pl.when(i % 2 == 1)
    def _():
        score_chunk(i - 1, 0, False)

    score_chunk(i, 1, True)

    def count_ge16(plane, cands):
        cands16 = [to16(c) for c in cands]

        def scan_chunk(j, parts):
            parts = list(parts)
            for ci, c16 in enumerate(cands16):
                acc = parts[ci]
                for r in range(t // BF16_ROWS):
                    acc = acc + jnp.where(plane[j, rows16(r), :] >= c16, jnp.int16(1), jnp.int16(0))
                parts[ci] = acc
            return tuple(parts)

        def scan_group(g, parts):
            for u in range(SCAN_GROUP):
                parts = scan_chunk(g * SCAN_GROUP + u, parts)
            return parts

        zero = jnp.zeros((BF16_ROWS, t), I16)
        parts = lax.fori_loop(0, n_groups, scan_group, tuple(zero for _ in cands))
        parts = lax.fori_loop(n_groups * SCAN_GROUP, n_chunks, scan_chunk, parts)
        return [jnp.sum(p.astype(I32), axis=0, keepdims=True) for p in parts]

    def select16(plane, want):
        def bit_step(it, state):
            cur_u, c_ge, c_gt = state
            cand_u = cur_u | (jnp.int32(1) << (15 - it))
            cnt, = count_ge16(plane, [cand_u + I16_MIN])
            ok = cnt >= want
            return jnp.where(ok, cand_u, cur_u), jnp.where(ok, cnt, c_ge), jnp.where(ok, c_gt, cnt)

        everything = jnp.full((1, t), n_chunks * t, I32)
        cur_u, c_ge, c_gt = lax.fori_loop(
            0, 16, bit_step, (jnp.zeros((1, t), I32), everything, jnp.zeros((1, t), I32)))
        return cur_u + I16_MIN, c_ge, c_gt

    th, c_hi_ge, c_above = select16(hi_scr, jnp.full((1, t), topk, I32))
    th16 = to16(th)

    def low_candidates(j, carry):
        for r in range(t // BF16_ROWS):
            lo_scr[j, rows16(r), :] = jnp.where(hi_scr[j, rows16(r), :] == th16,
                                                lo_scr[j, rows16(r), :], jnp.int16(I16_MIN))
        return carry

    lax.fori_loop(0, n_chunks, low_candidates, 0)

    tl, c_lo_ge, c_lo_gt = select16(lo_scr, topk - c_above)
    c_ge = jnp.where(tl == I16_MIN, c_hi_ge, c_above + c_lo_ge)
    need = topk - (c_above + c_lo_gt)
    has_thr = th > I16_MIN
    tie_split = (c_ge > topk) & has_thr
    any_split = jnp.max(jnp.where(tie_split, 1, 0)) > 0

    tl16 = to16(tl)
    keep = jnp.zeros((BF16_ROWS, t), BF16)
    drop = jnp.full((BF16_ROWS, t), NEG_BIG, BF16)
    one16 = jnp.ones((BF16_ROWS, t), I16)
    zero16 = jnp.zeros((BF16_ROWS, t), I16)

    def lanes16(pred):
        return to16(jnp.where(pred, 1, 0)) > 0

    @pl.when(jnp.logical_not(any_split))
    def _():
        tie = jnp.where(lanes16(has_thr), keep, drop)

        def mask_chunk(j, carry):
            for r in range(t // BF16_ROWS):
                hs = hi_scr[j, rows16(r), :]
                ls = lo_scr[j, rows16(r), :]
                at_th = jnp.where(ls > tl16, keep, jnp.where(ls == tl16, tie, drop))
                mb = jnp.where(hs > th16, keep, jnp.where(hs == th16, at_th, drop))
                mask_scr[j, rows16(r), :] = mb.astype(F32)
            return carry

        lax.fori_loop(0, n_chunks, mask_chunk, 0)

    @pl.when(any_split)
    def _():
        row16 = lax.broadcasted_iota(I32, (BF16_ROWS, t), 0).astype(I16)

        def tie_tile(j, r):
            hs = hi_scr[j, rows16(r), :]
            ls = lo_scr[j, rows16(r), :]
            return jnp.where(hs == th16, jnp.where(ls == tl16, one16, zero16), zero16)

        def chunk_total(j, carry):
            acc = zero16
            for r in range(t // BF16_ROWS):
                acc = acc + tie_tile(j, r)
            tot = jnp.sum(acc.astype(I32), axis=0, keepdims=True)
            tot_scr[j] = jnp.broadcast_to(tot, (SUBLANES, t))
            return carry

        lax.fori_loop(0, n_chunks, chunk_total, 0)

        def locate_chunk(j, carry):
            cum, n_before, ties_before = carry
            tot = tot_scr[j][0:1, :]
            cum = cum + tot
            before = cum < need
            return cum, n_before + jnp.where(before, 1, 0), ties_before + jnp.where(before, tot, 0)

        zero1 = jnp.zeros((1, t), I32)
        _, cut_chunk, ties_before = lax.fori_loop(0, n_chunks, locate_chunk, (zero1, zero1, zero1))

        col_scr[...] = jnp.zeros(col_scr.shape, I16)

        def gather_chunk(j, carry):
            here = lanes16(cut_chunk == j)
            for r in range(t // BF16_ROWS):
                col_scr[rows16(r), :] = jnp.where(here, tie_tile(j, r), col_scr[rows16(r), :])
            return carry

        lax.fori_loop(0, n_chunks, gather_chunk, 0)

        wanted = need - ties_before

        def row_bit(it, cur):
            cand = cur | (jnp.int32(1) << (tile_bits - 1 - it))
            cand16 = to16(cand)
            acc = zero16
            for r in range(t // BF16_ROWS):
                acc = acc + jnp.where(row16 + jnp.int16(BF16_ROWS * r) < cand16,
                                      col_scr[rows16(r), :], zero16)
            cnt = jnp.sum(acc.astype(I32), axis=0, keepdims=True)
            return jnp.where(cnt < wanted, cand, cur)

        cut_row16 = to16(lax.fori_loop(0, tile_bits, row_bit, zero1))
        cut_chunk = jnp.where(tie_split, cut_chunk, jnp.where(has_thr, jnp.int32(2 ** 30), -1))

        def mask_chunk(j, carry):
            whole = lanes16(j < cut_chunk)
            partial = lanes16(j == cut_chunk)
            for r in range(t // BF16_ROWS):
                hs = hi_scr[j, rows16(r), :]
                ls = lo_scr[j, rows16(r), :]
                in_cut = jnp.where(row16 + jnp.int16(BF16_ROWS * r) <= cut_row16, keep, drop)
                tie = jnp.where(whole, keep, jnp.where(partial, in_cut, drop))
                at_th = jnp.where(ls > tl16, keep, jnp.where(ls == tl16, tie, drop))
                mb = jnp.where(hs > th16, keep, jnp.where(hs == th16, at_th, drop))
                mask_scr[j, rows16(r), :] = mb.astype(F32)
            return carry

        lax.fori_loop(0, n_chunks, mask_chunk, 0)

    m_scr[...] = jnp.full(m_scr.shape, -jnp.inf, F32)
    acc_scr[...] = jnp.zeros(acc_scr.shape, F32)

    def attend(j, slot0, bias_sel, live=None):
        for h in range(N_HEADS):
            pair = slice(LANES * (h // 2), LANES * (h // 2) + LANES)
            kc = k_ref[0, pl.ds(pl.multiple_of(j * t, t), t), pair]
            mb = mask_scr[j]
            if live is not None:
                mb = jnp.where(live, mb, NEG_BIG)
            s = jnp.dot(kc, qpad_scr[h], preferred_element_type=F32) + mb
            if bias_sel is not None:
                s = s + bias_ref[bias_sel, h]
            s_scr[slot0 + h] = s
            m_old = m_scr[h:h + 1, :]
            m_new = jnp.maximum(m_old, jnp.max(s, axis=0, keepdims=True))
            alpha_scr[slot0 + h:slot0 + h + 1, :] = jnp.exp2(m_old - m_new)
            m_scr[h:h + 1, :] = m_new
        for h in range(N_HEADS):
            rows = slice(V_ROWS * h, V_ROWS * (h + 1))
            p = jnp.exp2((s_scr[slot0 + h] - m_scr[h:h + 1, :]).astype(BF16))
            pv = jnp.dot(vt_ref[0, j, rows, :], p, preferred_element_type=F32)
            acc_scr[rows, :] = alpha_scr[slot0 + h:slot0 + h + 1, :] * acc_scr[rows, :] + pv

    n_far = jnp.maximum(i - 1, 0)

    def far_pair(p, carry):
        attend(2 * p, 0, None)
        attend(2 * p + 1, N_HEADS, None)
        return carry

    lax.fori_loop(0, n_far // 2, far_pair, 0)

    @pl.when(n_far % 2 == 1)
    def _():
        attend(n_far - 1, 0, None)

    attend(jnp.maximum(i - 1, 0), 0, 1, live=i >= 1)
    attend(i, N_HEADS, 0)

    outs = []
    for h in range(N_HEADS):
        base = V_ROWS * h
        inv_l = 1.0 / acc_scr[base + HEAD_DIM:base + HEAD_DIM + 1, :]
        outs.append(acc_scr[base:base + HEAD_DIM, :] * inv_l)
    o_ref[0] = jnp.concatenate(outs, axis=0).T.astype(BF16)


def _attention(qt, qit, wit, k, vt4, ki2, bias, *, tile, topk):
    b, s, _ = k.shape
    nq = s // tile
    assert tile >= topk and tile >= _BIAS_REACH and s % tile == 0 and (tile & (tile - 1)) == 0
    kern = functools.partial(_attn_kernel, tile=tile, topk=topk, tile_bits=tile.bit_length() - 1,
                             wi_scale=(N_IDX_HEADS ** -0.5) * (IDX_DIM ** -0.5))
    resident = pl.Buffered(1)
    return pl.pallas_call(
        kern,
        grid=(b, nq),
        in_specs=[
            pl.BlockSpec((1, ATTN_WIDTH, tile), lambda bb, i: (bb, 0, i)),
            pl.BlockSpec((1, ATTN_WIDTH, tile), lambda bb, i: (bb, 0, i)),
            pl.BlockSpec((1, N_IDX_HEADS, tile), lambda bb, i: (bb, 0, i)),
            pl.BlockSpec((1, s, ATTN_WIDTH), lambda bb, i: (bb, 0, 0), pipeline_mode=resident),
            pl.BlockSpec((1, nq, N_HEADS * V_ROWS, tile), lambda bb, i: (bb, 0, 0, 0),
                         pipeline_mode=resident),
            pl.BlockSpec((1, s, LANES), lambda bb, i: (bb, 0, 0), pipeline_mode=resident),
            pl.BlockSpec((2, N_HEADS, tile, tile), lambda bb, i: (0, 0, 0, 0),
                         pipeline_mode=resident),
        ],
        out_specs=pl.BlockSpec((1, tile, ATTN_WIDTH), lambda bb, i: (bb, i, 0)),
        out_shape=jax.ShapeDtypeStruct((b, s, ATTN_WIDTH), BF16),
        scratch_shapes=[
            pltpu.VMEM((nq, tile, tile), I16),
            pltpu.VMEM((nq, tile, tile), I16),
            pltpu.VMEM((nq, tile, tile), F32),
            pltpu.VMEM((2 * N_HEADS, tile, tile), F32),
            pltpu.VMEM((N_HEADS * V_ROWS, tile), F32),
            pltpu.VMEM((N_HEADS, tile), F32),
            pltpu.VMEM((2 * N_HEADS, tile), F32),
            pltpu.VMEM((nq, SUBLANES, tile), I32),
            pltpu.VMEM((tile, tile), I16),
            pltpu.VMEM((N_HEADS, LANES, tile), BF16),
            pltpu.VMEM((N_IDX_HEADS, LANES, tile), BF16),
        ],
        compiler_params=pltpu.CompilerParams(
            dimension_semantics=("arbitrary", "arbitrary"), vmem_limit_bytes=VMEM_LIMIT_BYTES),
        name="dsa_attention",
    )(qt, qit, wit, k, vt4, ki2, bias)


def _conv_kernel(u_ref, dw_ref, db_ref, g_ref, b_ref, o_ref, h_scr, shift_scr, *, tc, sub):
    ch = o_ref.shape[-1]
    halo = CONV_HALO

    @pl.when(pl.program_id(1) == 0)
    def _():
        h_scr[0:halo, :] = jnp.zeros((halo, ch), F32)

    u = u_ref[0]
    h_scr[halo:halo + tc, :] = u[:, :ch] * _sigmoid(u[:, ch:])
    first = halo - (CONV_WIDTH - 1)

    span = tc + halo - SUBLANES
    for s in range(1, SUBLANES):
        shift_scr[s - 1, 0:span, :] = h_scr[s:s + span, :]

    for r in range(tc // sub):
        base = r * sub
        acc = jnp.zeros((sub, ch), F32)
        for j in range(CONV_WIDTH):
            s = (first + j) % SUBLANES
            lo = base + first + j - s
            rows = h_scr[lo:lo + sub, :] if s == 0 else shift_scr[s - 1, lo:lo + sub, :]
            acc = acc + dw_ref[j:j + 1, :] * rows
        acc = acc + db_ref[...]
        mu = jnp.mean(acc, axis=-1, keepdims=True)
        cen = acc - mu
        var = jnp.mean(cen * cen, axis=-1, keepdims=True)
        y = cen * lax.rsqrt(var + EPS) * g_ref[...] + b_ref[...]
        o_ref[0, base:base + sub, :] = (y * _sigmoid(y)).astype(o_ref.dtype)
    h_scr[0:halo, :] = h_scr[tc:tc + halo, :]


def _conformer_conv(u, dw, db, g, bb, *, tc, sub=64):
    b, s, c2 = u.shape
    ch = c2 // 2
    const = lambda bi, i: (0, 0)
    return pl.pallas_call(
        functools.partial(_conv_kernel, tc=tc, sub=sub),
        grid=(b, s // tc),
        in_specs=[
            pl.BlockSpec((1, tc, c2), lambda bi, i: (bi, i, 0)),
            pl.BlockSpec(dw.shape, const),
            pl.BlockSpec((1, ch), const),
            pl.BlockSpec((1, ch), const),
            pl.BlockSpec((1, ch), const),
        ],
        out_specs=pl.BlockSpec((1, tc, ch), lambda bi, i: (bi, i, 0)),
        out_shape=jax.ShapeDtypeStruct((b, s, ch), BF16),
        scratch_shapes=[pltpu.VMEM((tc + CONV_HALO, ch), F32),
                        pltpu.VMEM((SUBLANES - 1, tc + CONV_HALO, ch), F32)],
        compiler_params=pltpu.CompilerParams(
            dimension_semantics=("arbitrary", "arbitrary"), vmem_limit_bytes=VMEM_LIMIT_BYTES),
        name="conformer_conv",
    )(u, dw, db, g, bb)


def _mix_kernel(x_ref, a_ref, c_ref, gg_ref, wa_ref, wc_ref, wm_ref, o_ref):
    d = x_ref.shape[-1]
    y_a = jnp.dot(a_ref[...], wa_ref[...], preferred_element_type=F32)
    y_b = jnp.dot(c_ref[...], wc_ref[...], preferred_element_type=F32)
    gg = gg_ref[...]
    merged = _sigmoid(gg[:, :d]) * y_a + _sigmoid(gg[:, d:]) * y_b
    o_ref[...] = x_ref[...] + jnp.dot(merged.astype(BF16), wm_ref[...], preferred_element_type=F32)


def _mix(x2d, attn, conv, gg, wa, wc, wm, *, tm):
    n, d = x2d.shape
    row = lambda i: (i, 0)
    const = lambda i: (0, 0)
    return pl.pallas_call(
        _mix_kernel,
        grid=(n // tm,),
        in_specs=[
            pl.BlockSpec((tm, d), row),
            pl.BlockSpec((tm, attn.shape[1]), row),
            pl.BlockSpec((tm, conv.shape[1]), row),
            pl.BlockSpec((tm, gg.shape[1]), row),
            pl.BlockSpec(wa.shape, const),
            pl.BlockSpec(wc.shape, const),
            pl.BlockSpec(wm.shape, const),
        ],
        out_specs=pl.BlockSpec((tm, d), row),
        out_shape=jax.ShapeDtypeStruct((n, d), F32),
        compiler_params=pltpu.CompilerParams(
            dimension_semantics=("parallel",), vmem_limit_bytes=VMEM_LIMIT_BYTES),
        name="gated_mix",
    )(x2d, attn, conv, gg, wa, wc, wm)


def _ffn_kernel(x_ref, g_ref, wg_ref, wu_ref, wo_ref, fg_ref, o_ref, h_scr, acc_scr, *, final_norm):
    j = pl.program_id(1)

    @pl.when(j == 0)
    def _():
        x = x_ref[...]
        ms = jnp.mean(x * x, axis=-1, keepdims=True)
        h_scr[...] = ((x * lax.rsqrt(ms + EPS)) * g_ref[...]).astype(BF16)
        acc_scr[...] = jnp.zeros(acc_scr.shape, F32)

    h = h_scr[...]
    gate = jnp.dot(h, wg_ref[...], preferred_element_type=F32)
    up = jnp.dot(h, wu_ref[...], preferred_element_type=F32)
    act = (gate * _sigmoid(gate) * up).astype(BF16)
    acc_scr[...] += jnp.dot(act, wo_ref[...], preferred_element_type=F32)

    @pl.when(j == pl.num_programs(1) - 1)
    def _():
        y = x_ref[...] + acc_scr[...]
        if final_norm:
            ms = jnp.mean(y * y, axis=-1, keepdims=True)
            y = (y * lax.rsqrt(ms + EPS)) * fg_ref[...]
        o_ref[...] = y


def _ffn(x2d, g, w_gate, w_up, w_out, fg, *, tm, tf, final_norm):
    n, d = x2d.shape
    dff = w_gate.shape[1]
    return pl.pallas_call(
        functools.partial(_ffn_kernel, final_norm=final_norm),
        grid=(n // tm, dff // tf),
        in_specs=[
            pl.BlockSpec((tm, d), lambda i, j: (i, 0)),
            pl.BlockSpec((1, d), lambda i, j: (0, 0)),
            pl.BlockSpec((d, tf), lambda i, j: (0, j)),
            pl.BlockSpec((d, tf), lambda i, j: (0, j)),
            pl.BlockSpec((tf, d), lambda i, j: (j, 0)),
            pl.BlockSpec((1, d), lambda i, j: (0, 0)),
        ],
        out_specs=pl.BlockSpec((tm, d), lambda i, j: (i, 0)),
        out_shape=jax.ShapeDtypeStruct((n, d), F32),
        scratch_shapes=[pltpu.VMEM((tm, d), BF16), pltpu.VMEM((tm, d), F32)],
        compiler_params=pltpu.CompilerParams(
            dimension_semantics=("parallel", "arbitrary"), vmem_limit_bytes=VMEM_LIMIT_BYTES),
        name="swiglu_ffn",
    )(x2d, g, w_gate, w_up, w_out, fg)


def _pick_tile(n, want):
    t = min(n, want)
    while n % t:
        t //= 2
    return t


def kernel(x, rel_bias, mix_norm, w_in, w_attn_out, dw_kernel, dw_bias, conv_norm_g, conv_norm_b,
           w_conv_out, w_mix_out, ffn_norm, w_ffn_in, w_ffn_out, final_norm):
    b, s, d = x.shape
    depth = w_in.shape[0]
    n = b * s
    conv_ch = d // 2
    d_ff = w_ffn_out.shape[1]
    topk = min(TOPK_MAX, s // 4)
    tile = ATTN_TILE
    nq = s // tile

    sizes = (ATTN_WIDTH, ATTN_WIDTH, ATTN_WIDTH, N_IDX_HEADS * IDX_DIM, IDX_DIM, N_IDX_HEADS,
             2 * conv_ch, d, d)
    offs = np.concatenate([[0], np.cumsum(sizes)])
    o_ki, o_wi, o_u, o_g = offs[4], offs[5], offs[6], offs[7]

    bias = _bias_tiles(rel_bias.astype(F32), tile)

    x2d = x.reshape(n, d).astype(F32)
    for l in range(depth):
        w = w_in[l]
        wa = w[:, :o_ki].astype(BF16)
        w_ki = w[:, o_ki:o_wi]
        w_wi = w[:, o_wi:o_u]
        wkw = jnp.concatenate(
            [w_ki, w_ki, w_wi, jnp.zeros((d, LANES - N_IDX_HEADS), w.dtype)], axis=1).astype(BF16)
        wu = w[:, o_u:o_g].astype(BF16)
        wg = w[:, o_g:].astype(BF16)

        qt, k, vt4, qit, ki2, wi, u, gg = _inproj(
            x2d, mix_norm[l].reshape(1, d).astype(F32), wa, wkw, wu, wg,
            batch=b, tm=_pick_tile(s, 512), tile=tile)

        wit = wi[:, :N_IDX_HEADS].reshape(b, s, N_IDX_HEADS).transpose(0, 2, 1)
        attn = _attention(qt, qit, wit, k.reshape(b, s, -1), vt4,
                          ki2.reshape(b, s, -1), bias, tile=tile, topk=topk)

        conv = _conformer_conv(
            u.reshape(b, s, -1), dw_kernel[l].astype(F32), dw_bias[l].reshape(1, -1).astype(F32),
            conv_norm_g[l].reshape(1, -1).astype(F32), conv_norm_b[l].reshape(1, -1).astype(F32),
            tc=_pick_tile(s, 512))

        x2d = _mix(x2d, attn.reshape(n, -1), conv.reshape(n, -1), gg,
                   w_attn_out[l].astype(BF16), w_conv_out[l].astype(BF16), w_mix_out[l].astype(BF16),
                   tm=_pick_tile(n, 512))

        wf = w_ffn_in[l]
        x2d = _ffn(x2d, ffn_norm[l].reshape(1, d).astype(F32),
                   wf[:, :d_ff].astype(BF16), wf[:, d_ff:].astype(BF16), w_ffn_out[l].astype(BF16),
                   final_norm.reshape(1, d).astype(F32),
                   tm=_pick_tile(n, 1024), tf=256, final_norm=(l == depth - 1))
    return x2d.reshape(b, s, d).astype(x.dtype)
```

```python
import functools
import math

import numpy as np
import jax
import jax.numpy as jnp
from jax import lax
from jax.experimental import pallas as pl
from jax.experimental.pallas import tpu as pltpu

F32 = jnp.float32
BF16 = jnp.bfloat16
I32 = jnp.int32
I16 = jnp.int16

N_HEADS = 8
HEAD_DIM = 64
ATTN_WIDTH = N_HEADS * HEAD_DIM
N_IDX_HEADS = 8
IDX_DIM = 64
TOPK_MAX = 256
CONV_WIDTH = 31
N_BUCKETS = 32
MAX_DISTANCE = 128
EPS = 1e-6

LANES = 128
SUBLANES = 8
BF16_ROWS = 16
VMEM_LIMIT_BYTES = 56 * 1024 * 1024
LOG2E = math.log2(math.e)
NEG_BIG = -1e30
INT_MIN = -(2 ** 31)
I16_MIN = -(2 ** 15)
I16_MAX = 2 ** 15 - 1

ATTN_TILE = 256
V_ROWS = HEAD_DIM + BF16_ROWS
SCAN_GROUP = 8
FAR_GROUP = 4
CONV_HALO = 32


def _sigmoid(x):
    return 1.0 / (1.0 + jnp.exp(-x))


def _t5_bucket_ranges():
    max_exact = N_BUCKETS // 2
    buckets = []
    for n in range(0, 4 * MAX_DISTANCE):
        if n < max_exact:
            buckets.append(n)
        else:
            b = max_exact + int(math.log(n / max_exact) / math.log(MAX_DISTANCE / max_exact)
                                * (N_BUCKETS - max_exact))
            buckets.append(min(b, N_BUCKETS - 1))
    ranges = []
    for b in range(N_BUCKETS - 1):
        ns = [n for n, bb in enumerate(buckets) if bb == b]
        ranges.append((min(ns), max(ns)))
    return ranges


_BUCKET_RANGES = _t5_bucket_ranges()
_BIAS_REACH = _BUCKET_RANGES[-1][1] + 1


def _inproj_kernel(x_ref, g_ref, wa_ref, wkw_ref, wu_ref, wg_ref,
                   qt_ref, k_ref, vt_ref, qit_ref, ki_ref, wi_ref, u_ref, gg_ref, t_scr,
                   *, q_scale, tile):
    x = x_ref[...]
    tm = x.shape[0]
    ms = jnp.mean(x * x, axis=-1, keepdims=True)
    h = ((x * lax.rsqrt(ms + EPS)) * g_ref[...]).astype(BF16)

    def mm(w):
        return jnp.dot(h, w, preferred_element_type=F32)

    w = ATTN_WIDTH

    def transposed(y):
        t_scr[...] = y
        return t_scr[...].T.astype(BF16)

    qt_ref[0] = transposed(mm(wa_ref[:, 0:w]) * q_scale)
    k_ref[...] = mm(wa_ref[:, w:2 * w]).astype(BF16)
    qit_ref[0] = transposed(mm(wa_ref[:, 3 * w:4 * w]))
    vt = transposed(mm(wa_ref[:, 2 * w:3 * w]))
    tail_row = lax.broadcasted_iota(I32, (V_ROWS - HEAD_DIM, tile), 0)
    tail = jnp.where(tail_row == 0, 1.0, 0.0).astype(BF16)
    for c in range(tm // tile):
        for hd in range(N_HEADS):
            base = V_ROWS * hd
            vt_ref[0, c, base:base + HEAD_DIM, :] = vt[HEAD_DIM * hd:HEAD_DIM * (hd + 1),
                                                       tile * c:tile * (c + 1)]
            vt_ref[0, c, base + HEAD_DIM:base + V_ROWS, :] = tail
    kw = mm(wkw_ref[...])
    ki_ref[...] = kw[:, :LANES].astype(BF16)
    wi_ref[...] = kw[:, LANES:]
    u_ref[...] = mm(wu_ref[...])
    gg_ref[...] = mm(wg_ref[...])


def _inproj(x2d, g, wa, wkw, wu, wg, *, batch, tm, tile):
    n, d = x2d.shape
    seq = n // batch
    assert seq % tm == 0 and tm % tile == 0
    spb = seq // tm
    grid = (n // tm,)
    row = lambda i: (i, 0)
    const = lambda i: (0, 0)
    feat_major = lambda i: (i // spb, 0, i % spb)
    out_shape = (
        jax.ShapeDtypeStruct((batch, ATTN_WIDTH, seq), BF16),
        jax.ShapeDtypeStruct((n, ATTN_WIDTH), BF16),
        jax.ShapeDtypeStruct((batch, seq // tile, N_HEADS * V_ROWS, tile), BF16),
        jax.ShapeDtypeStruct((batch, ATTN_WIDTH, seq), BF16),
        jax.ShapeDtypeStruct((n, LANES), BF16),
        jax.ShapeDtypeStruct((n, LANES), F32),
        jax.ShapeDtypeStruct((n, wu.shape[1]), F32),
        jax.ShapeDtypeStruct((n, wg.shape[1]), F32),
    )
    out_specs = [
        pl.BlockSpec((1, ATTN_WIDTH, tm), feat_major),
        pl.BlockSpec((tm, ATTN_WIDTH), row),
        pl.BlockSpec((1, tm // tile, N_HEADS * V_ROWS, tile), lambda i: (i // spb, i % spb, 0, 0)),
        pl.BlockSpec((1, ATTN_WIDTH, tm), feat_major),
        pl.BlockSpec((tm, LANES), row),
        pl.BlockSpec((tm, LANES), row),
        pl.BlockSpec((tm, wu.shape[1]), row),
        pl.BlockSpec((tm, wg.shape[1]), row),
    ]
    return pl.pallas_call(
        functools.partial(_inproj_kernel, q_scale=HEAD_DIM ** -0.5 * LOG2E, tile=tile),
        grid=grid,
        in_specs=[
            pl.BlockSpec((tm, d), row),
            pl.BlockSpec((1, d), const),
            pl.BlockSpec(wa.shape, const),
            pl.BlockSpec(wkw.shape, const),
            pl.BlockSpec(wu.shape, const),
            pl.BlockSpec(wg.shape, const),
        ],
        out_specs=out_specs,
        out_shape=out_shape,
        scratch_shapes=[pltpu.VMEM((tm, ATTN_WIDTH), F32)],
        compiler_params=pltpu.CompilerParams(
            dimension_semantics=("parallel",), vmem_limit_bytes=VMEM_LIMIT_BYTES),
        name="inproj",
    )(x2d, g, wa, wkw, wu, wg)


def _bias_kernel(tab_ref, out_ref, *, tile):
    r = lax.broadcasted_iota(I32, (tile, tile), 0)
    c = lax.broadcasted_iota(I32, (tile, tile), 1)
    for w in range(2):
        dist = c - r + w * tile
        for h in range(N_HEADS):
            far = tab_ref[N_BUCKETS - 1, h]
            val = jnp.zeros((tile, tile), F32)
            for b, (lo, hi) in enumerate(_BUCKET_RANGES):
                hit = (dist == lo) if lo == hi else ((dist >= lo) & (dist <= hi))
                val = jnp.where(hit, (tab_ref[b, h] - far) * LOG2E, val)
            out_ref[w, h] = val


def _bias_tiles(rel_bias, tile):
    return pl.pallas_call(
        functools.partial(_bias_kernel, tile=tile),
        in_specs=[pl.BlockSpec(memory_space=pltpu.SMEM)],
        out_specs=pl.BlockSpec(memory_space=pltpu.VMEM),
        out_shape=jax.ShapeDtypeStruct((2, N_HEADS, tile, tile), F32),
        name="bias_tiles",
    )(rel_bias)


def _order_key(x):
    return x ^ ((x >> 31) & 0x7FFFFFFF)


def _attn_kernel(qt_ref, qit_ref, wi_ref, k_ref, vt_ref, ki_ref, bias_ref, o_ref,
                 hi_scr, lo_scr, mask_scr, s_scr, acc_scr, m_scr, alpha_scr, tot_scr, col_scr,
                 qpad_scr, qipad_scr, *, tile, topk, tile_bits, wi_scale):
    t = tile
    i = pl.program_id(1)
    n_chunks = i + 1
    n_groups = n_chunks // SCAN_GROUP

    def fold(x, op):
        return op(x.reshape(t // SUBLANES, SUBLANES, t), axis=0)

    def rows16(r):
        return slice(BF16_ROWS * r, BF16_ROWS * (r + 1))

    def to16(x):
        return jnp.broadcast_to(x, (BF16_ROWS, t)).astype(I16)

    feat = lax.broadcasted_iota(I32, (LANES, t), 0)
    for h in range(N_HEADS):
        pair = slice(LANES * (h // 2), LANES * (h // 2) + LANES)
        lo = HEAD_DIM * (h % 2)
        in_head = (feat >= lo) & (feat < lo + HEAD_DIM)
        qpad_scr[h] = jnp.where(in_head, qt_ref[0, pair, :], 0).astype(BF16)
        qipad_scr[h] = jnp.where(in_head, qit_ref[0, pair, :], 0).astype(BF16)

    w_all = wi_ref[0] * wi_scale
    row = lax.broadcasted_iota(I32, (t, t), 0)
    col = lax.broadcasted_iota(I32, (t, t), 1)

    def score_chunk(j, slot, diagonal):
        kij = ki_ref[0, pl.ds(pl.multiple_of(j * t, t), t), :]
        for hp in range(N_IDX_HEADS // 2):
            ha, hb = 2 * hp, 2 * hp + 1
            la = jnp.dot(kij, qipad_scr[ha], preferred_element_type=F32)
            lb = jnp.dot(kij, qipad_scr[hb], preferred_element_type=F32)
            term = (jnp.maximum(la, 0.0) * w_all[ha:ha + 1, :]
                    + jnp.maximum(lb, 0.0) * w_all[hb:hb + 1, :])
            if hp == 0:
                s_scr[slot] = term
            elif hp < N_IDX_HEADS // 2 - 1:
                s_scr[slot] = s_scr[slot] + term
            else:
                score = s_scr[slot] + term
                score = jnp.where(score == 0.0, 0.0, score)
                skey = _order_key(pltpu.bitcast(score, I32))
                if diagonal:
                    skey = jnp.where(row <= col, skey, INT_MIN)
                hi_scr[j] = (skey >> 16).astype(I16)
                lo_scr[j] = ((skey & 0xFFFF) + I16_MIN).astype(I16)

    def score_group(size, first):
        def body(g, carry):
            for u in range(size):
                score_chunk(first + g * size + u, u, False)
            return carry
        return body

    n_quads = i // 4
    n_twos = (i - 4 * n_quads) // 2
    lax.fori_loop(0, n_quads, score_group(4, 0), 0)
    lax.fori_loop(0, n_twos, score_group(2, 4 * n_quads), 0)

    @pl.when(i % 2 == 1)
    def _():
        score_chunk(i - 1, 0, False)

    score_chunk(i, 1, True)

    def count_ge16(plane, cands):
        cands16 = [to16(c) for c in cands]

        def scan_chunk(j, parts):
            parts = list(parts)
            for ci, c16 in enumerate(cands16):
                acc = parts[ci]
                for r in range(t // BF16_ROWS):
                    acc = acc + jnp.where(plane[j, rows16(r), :] >= c16, jnp.int16(1), jnp.int16(0))
                parts[ci] = acc
            return tuple(parts)

        def scan_group(size, first):
            def body(g, parts):
                for u in range(size):
                    parts = scan_chunk(first + g * size + u, parts)
                return parts
            return body

        zero = jnp.zeros((BF16_ROWS, t), I16)
        half = SCAN_GROUP // 2
        done = n_groups * SCAN_GROUP
        n_half = (n_chunks - done) // half
        parts = lax.fori_loop(0, n_groups, scan_group(SCAN_GROUP, 0), tuple(zero for _ in cands))
        parts = lax.fori_loop(0, n_half, scan_group(half, done), parts)
        parts = lax.fori_loop(done + n_half * half, n_chunks, scan_chunk, parts)
        return [jnp.sum(p.astype(I32), axis=0, keepdims=True) for p in parts]

    def select16(plane, want):
        def bit_step(it, state):
            cur_u, c_ge, c_gt = state
            cand_u = cur_u | (jnp.int32(1) << (15 - it))
            cnt, = count_ge16(plane, [cand_u + I16_MIN])
            ok = cnt >= want
            return jnp.where(ok, cand_u, cur_u), jnp.where(ok, cnt, c_ge), jnp.where(ok, c_gt, cnt)

        everything = jnp.full((1, t), n_chunks * t, I32)
        cur_u, c_ge, c_gt = lax.fori_loop(
            0, 16, bit_step, (jnp.zeros((1, t), I32), everything, jnp.zeros((1, t), I32)))
        return cur_u + I16_MIN, c_ge, c_gt

    th, c_hi_ge, c_above = select16(hi_scr, jnp.full((1, t), topk, I32))
    th16 = to16(th)

    def low_candidates(j, carry):
        for r in range(t // BF16_ROWS):
            lo_scr[j, rows16(r), :] = jnp.where(hi_scr[j, rows16(r), :] == th16,
                                                lo_scr[j, rows16(r), :], jnp.int16(I16_MIN))
        return carry

    lax.fori_loop(0, n_chunks, low_candidates, 0)

    tl, c_lo_ge, c_lo_gt = select16(lo_scr, topk - c_above)
    c_ge = jnp.where(tl == I16_MIN, c_hi_ge, c_above + c_lo_ge)
    need = topk - (c_above + c_lo_gt)
    has_thr = th > I16_MIN
    tie_split = (c_ge > topk) & has_thr
    any_split = jnp.max(jnp.where(tie_split, 1, 0)) > 0

    tl16 = to16(tl)
    keep = jnp.zeros((BF16_ROWS, t), BF16)
    drop = jnp.full((BF16_ROWS, t), NEG_BIG, BF16)
    one16 = jnp.ones((BF16_ROWS, t), I16)
    zero16 = jnp.zeros((BF16_ROWS, t), I16)

    def lanes16(pred):
        return to16(jnp.where(pred, 1, 0)) > 0

    @pl.when(jnp.logical_not(any_split))
    def _():
        tie = jnp.where(lanes16(has_thr), keep, drop)

        def mask_chunk(j, carry):
            for r in range(t // BF16_ROWS):
                hs = hi_scr[j, rows16(r), :]
                ls = lo_scr[j, rows16(r), :]
                at_th = jnp.where(ls > tl16, keep, jnp.where(ls == tl16, tie, drop))
                mb = jnp.where(hs > th16, keep, jnp.where(hs == th16, at_th, drop))
                mask_scr[j, rows16(r), :] = mb.astype(F32)
            return carry

        lax.fori_loop(0, n_chunks, mask_chunk, 0)

    @pl.when(any_split)
    def _():
        row16 = lax.broadcasted_iota(I32, (BF16_ROWS, t), 0).astype(I16)

        def tie_tile(j, r):
            hs = hi_scr[j, rows16(r), :]
            ls = lo_scr[j, rows16(r), :]
            return jnp.where(hs == th16, jnp.where(ls == tl16, one16, zero16), zero16)

        def chunk_total(j, carry):
            acc = zero16
            for r in range(t // BF16_ROWS):
                acc = acc + tie_tile(j, r)
            tot = jnp.sum(acc.astype(I32), axis=0, keepdims=True)
            tot_scr[j] = jnp.broadcast_to(tot, (SUBLANES, t))
            return carry

        lax.fori_loop(0, n_chunks, chunk_total, 0)

        def locate_chunk(j, carry):
            cum, n_before, ties_before = carry
            tot = tot_scr[j][0:1, :]
            cum = cum + tot
            before = cum < need
            return cum, n_before + jnp.where(before, 1, 0), ties_before + jnp.where(before, tot, 0)

        zero1 = jnp.zeros((1, t), I32)
        _, cut_chunk, ties_before = lax.fori_loop(0, n_chunks, locate_chunk, (zero1, zero1, zero1))

        col_scr[...] = jnp.zeros(col_scr.shape, I16)

        def gather_chunk(j, carry):
            here = lanes16(cut_chunk == j)
            for r in range(t // BF16_ROWS):
                col_scr[rows16(r), :] = jnp.where(here, tie_tile(j, r), col_scr[rows16(r), :])
            return carry

        lax.fori_loop(0, n_chunks, gather_chunk, 0)

        wanted = need - ties_before

        def row_bit(it, cur):
            cand = cur | (jnp.int32(1) << (tile_bits - 1 - it))
            cand16 = to16(cand)
            acc = zero16
            for r in range(t // BF16_ROWS):
                acc = acc + jnp.where(row16 + jnp.int16(BF16_ROWS * r) < cand16,
                                      col_scr[rows16(r), :], zero16)
            cnt = jnp.sum(acc.astype(I32), axis=0, keepdims=True)
            return jnp.where(cnt < wanted, cand, cur)

        cut_row16 = to16(lax.fori_loop(0, tile_bits, row_bit, zero1))
        cut_chunk = jnp.where(tie_split, cut_chunk, jnp.where(has_thr, jnp.int32(2 ** 30), -1))

        def mask_chunk(j, carry):
            whole = lanes16(j < cut_chunk)
            partial = lanes16(j == cut_chunk)
            for r in range(t // BF16_ROWS):
                hs = hi_scr[j, rows16(r), :]
                ls = lo_scr[j, rows16(r), :]
                in_cut = jnp.where(row16 + jnp.int16(BF16_ROWS * r) <= cut_row16, keep, drop)
                tie = jnp.where(whole, keep, jnp.where(partial, in_cut, drop))
                at_th = jnp.where(ls > tl16, keep, jnp.where(ls == tl16, tie, drop))
                mb = jnp.where(hs > th16, keep, jnp.where(hs == th16, at_th, drop))
                mask_scr[j, rows16(r), :] = mb.astype(F32)
            return carry

        lax.fori_loop(0, n_chunks, mask_chunk, 0)

    m_scr[...] = jnp.full(m_scr.shape, -jnp.inf, F32)
    acc_scr[...] = jnp.zeros(acc_scr.shape, F32)

    def attend(j, slot0, bias_sel, live=None):
        for h in range(N_HEADS):
            pair = slice(LANES * (h // 2), LANES * (h // 2) + LANES)
            kc = k_ref[0, pl.ds(pl.multiple_of(j * t, t), t), pair]
            mb = mask_scr[j]
            if live is not None:
                mb = jnp.where(live, mb, NEG_BIG)
            s = jnp.dot(kc, qpad_scr[h], preferred_element_type=F32) + mb
            if bias_sel is not None:
                s = s + bias_ref[bias_sel, h]
            s_scr[slot0 + h] = s
            m_old = m_scr[h:h + 1, :]
            m_new = jnp.maximum(m_old, jnp.max(s, axis=0, keepdims=True))
            alpha_scr[slot0 + h:slot0 + h + 1, :] = jnp.exp2(m_old - m_new)
            m_scr[h:h + 1, :] = m_new
        for h in range(N_HEADS):
            rows = slice(V_ROWS * h, V_ROWS * (h + 1))
            p = jnp.exp2((s_scr[slot0 + h] - m_scr[h:h + 1, :]).astype(BF16))
            pv = jnp.dot(vt_ref[0, j, rows, :], p, preferred_element_type=F32)
            acc_scr[rows, :] = alpha_scr[slot0 + h:slot0 + h + 1, :] * acc_scr[rows, :] + pv

    n_far = jnp.maximum(i - 1, 0)

    def far_group(size, first):
        def body(g, carry):
            for u in range(size):
                attend(first + g * size + u, N_HEADS * u, None)
            return carry
        return body

    n_big = n_far // FAR_GROUP
    lax.fori_loop(0, n_big, far_group(FAR_GROUP, 0), 0)
    lax.fori_loop(0, (n_far - FAR_GROUP * n_big) // 2, far_group(2, FAR_GROUP * n_big), 0)

    @pl.when(n_far % 2 == 1)
    def _():
        attend(n_far - 1, 0, None)

    attend(jnp.maximum(i - 1, 0), 0, 1, live=i >= 1)
    attend(i, N_HEADS, 0)

    outs = []
    for h in range(N_HEADS):
        base = V_ROWS * h
        inv_l = 1.0 / acc_scr[base + HEAD_DIM:base + HEAD_DIM + 1, :]
        outs.append(acc_scr[base:base + HEAD_DIM, :] * inv_l)
    o_ref[0] = jnp.concatenate(outs, axis=0).T.astype(BF16)


def _attention(qt, qit, wit, k, vt4, ki2, bias, *, tile, topk):
    b, s, _ = k.shape
    nq = s // tile
    assert tile >= topk and tile >= _BIAS_REACH and s % tile == 0 and (tile & (tile - 1)) == 0
    kern = functools.partial(_attn_kernel, tile=tile, topk=topk, tile_bits=tile.bit_length() - 1,
                             wi_scale=(N_IDX_HEADS ** -0.5) * (IDX_DIM ** -0.5))
    resident = pl.Buffered(1)
    return pl.pallas_call(
        kern,
        grid=(b, nq),
        in_specs=[
            pl.BlockSpec((1, ATTN_WIDTH, tile), lambda bb, i: (bb, 0, i)),
            pl.BlockSpec((1, ATTN_WIDTH, tile), lambda bb, i: (bb, 0, i)),
            pl.BlockSpec((1, N_IDX_HEADS, tile), lambda bb, i: (bb, 0, i)),
            pl.BlockSpec((1, s, ATTN_WIDTH), lambda bb, i: (bb, 0, 0), pipeline_mode=resident),
            pl.BlockSpec((1, nq, N_HEADS * V_ROWS, tile), lambda bb, i: (bb, 0, 0, 0),
                         pipeline_mode=resident),
            pl.BlockSpec((1, s, LANES), lambda bb, i: (bb, 0, 0), pipeline_mode=resident),
            pl.BlockSpec((2, N_HEADS, tile, tile), lambda bb, i: (0, 0, 0, 0),
                         pipeline_mode=resident),
        ],
        out_specs=pl.BlockSpec((1, tile, ATTN_WIDTH), lambda bb, i: (bb, i, 0)),
        out_shape=jax.ShapeDtypeStruct((b, s, ATTN_WIDTH), BF16),
        scratch_shapes=[
            pltpu.VMEM((nq, tile, tile), I16),
            pltpu.VMEM((nq, tile, tile), I16),
            pltpu.VMEM((nq, tile, tile), F32),
            pltpu.VMEM((FAR_GROUP * N_HEADS, tile, tile), F32),
            pltpu.VMEM((N_HEADS * V_ROWS, tile), F32),
            pltpu.VMEM((N_HEADS, tile), F32),
            pltpu.VMEM((FAR_GROUP * N_HEADS, tile), F32),
            pltpu.VMEM((nq, SUBLANES, tile), I32),
            pltpu.VMEM((tile, tile), I16),
            pltpu.VMEM((N_HEADS, LANES, tile), BF16),
            pltpu.VMEM((N_IDX_HEADS, LANES, tile), BF16),
        ],
        compiler_params=pltpu.CompilerParams(
            dimension_semantics=("arbitrary", "arbitrary"), vmem_limit_bytes=VMEM_LIMIT_BYTES),
        name="dsa_attention",
    )(qt, qit, wit, k, vt4, ki2, bias)


def _conv_kernel(u_ref, dw_ref, db_ref, g_ref, b_ref, o_ref, h_scr, shift_scr, *, tc, sub):
    ch = o_ref.shape[-1]
    halo = CONV_HALO

    @pl.when(pl.program_id(1) == 0)
    def _():
        h_scr[0:halo, :] = jnp.zeros((halo, ch), F32)

    u = u_ref[0]
    h_scr[halo:halo + tc, :] = u[:, :ch] * _sigmoid(u[:, ch:])
    first = halo - (CONV_WIDTH - 1)

    span = tc + halo - SUBLANES
    for s in range(1, SUBLANES):
        shift_scr[s - 1, 0:span, :] = h_scr[s:s + span, :]

    for r in range(tc // sub):
        base = r * sub
        acc = jnp.zeros((sub, ch), F32)
        for j in range(CONV_WIDTH):
            s = (first + j) % SUBLANES
            lo = base + first + j - s
            rows = h_scr[lo:lo + sub, :] if s == 0 else shift_scr[s - 1, lo:lo + sub, :]
            acc = acc + dw_ref[j:j + 1, :] * rows
        acc = acc + db_ref[...]
        mu = jnp.mean(acc, axis=-1, keepdims=True)
        cen = acc - mu
        var = jnp.mean(cen * cen, axis=-1, keepdims=True)
        y = cen * lax.rsqrt(var + EPS) * g_ref[...] + b_ref[...]
        o_ref[0, base:base + sub, :] = (y * _sigmoid(y)).astype(o_ref.dtype)
    h_scr[0:halo, :] = h_scr[tc:tc + halo, :]


def _conformer_conv(u, dw, db, g, bb, *, tc, sub=64):
    b, s, c2 = u.shape
    ch = c2 // 2
    const = lambda bi, i: (0, 0)
    return pl.pallas_call(
        functools.partial(_conv_kernel, tc=tc, sub=sub),
        grid=(b, s // tc),
        in_specs=[
            pl.BlockSpec((1, tc, c2), lambda bi, i: (bi, i, 0)),
            pl.BlockSpec(dw.shape, const),
            pl.BlockSpec((1, ch), const),
            pl.BlockSpec((1, ch), const),
            pl.BlockSpec((1, ch), const),
        ],
        out_specs=pl.BlockSpec((1, tc, ch), lambda bi, i: (bi, i, 0)),
        out_shape=jax.ShapeDtypeStruct((b, s, ch), BF16),
        scratch_shapes=[pltpu.VMEM((tc + CONV_HALO, ch), F32),
                        pltpu.VMEM((SUBLANES - 1, tc + CONV_HALO, ch), F32)],
        compiler_params=pltpu.CompilerParams(
            dimension_semantics=("arbitrary", "arbitrary"), vmem_limit_bytes=VMEM_LIMIT_BYTES),
        name="conformer_conv",
    )(u, dw, db, g, bb)


def _mix_kernel(x_ref, a_ref, c_ref, gg_ref, wa_ref, wc_ref, wm_ref, o_ref):
    d = x_ref.shape[-1]
    y_a = jnp.dot(a_ref[...], wa_ref[...], preferred_element_type=F32)
    y_b = jnp.dot(c_ref[...], wc_ref[...], preferred_element_type=F32)
    gg = gg_ref[...]
    merged = _sigmoid(gg[:, :d]) * y_a + _sigmoid(gg[:, d:]) * y_b
    o_ref[...] = x_ref[...] + jnp.dot(merged.astype(BF16), wm_ref[...], preferred_element_type=F32)


def _mix(x2d, attn, conv, gg, wa, wc, wm, *, tm):
    n, d = x2d.shape
    row = lambda i: (i, 0)
    const = lambda i: (0, 0)
    return pl.pallas_call(
        _mix_kernel,
        grid=(n // tm,),
        in_specs=[
            pl.BlockSpec((tm, d), row),
            pl.BlockSpec((tm, attn.shape[1]), row),
            pl.BlockSpec((tm, conv.shape[1]), row),
            pl.BlockSpec((tm, gg.shape[1]), row),
            pl.BlockSpec(wa.shape, const),
            pl.BlockSpec(wc.shape, const),
            pl.BlockSpec(wm.shape, const),
        ],
        out_specs=pl.BlockSpec((tm, d), row),
        out_shape=jax.ShapeDtypeStruct((n, d), F32),
        compiler_params=pltpu.CompilerParams(
            dimension_semantics=("parallel",), vmem_limit_bytes=VMEM_LIMIT_BYTES),
        name="gated_mix",
    )(x2d, attn, conv, gg, wa, wc, wm)


def _ffn_kernel(x_ref, g_ref, wg_ref, wu_ref, wo_ref, fg_ref, o_ref, h_scr, acc_scr, *, final_norm):
    j = pl.program_id(1)

    @pl.when(j == 0)
    def _():
        x = x_ref[...]
        ms = jnp.mean(x * x, axis=-1, keepdims=True)
        h_scr[...] = ((x * lax.rsqrt(ms + EPS)) * g_ref[...]).astype(BF16)
        acc_scr[...] = jnp.zeros(acc_scr.shape, F32)

    h = h_scr[...]
    gate = jnp.dot(h, wg_ref[...], preferred_element_type=F32)
    up = jnp.dot(h, wu_ref[...], preferred_element_type=F32)
    act = (gate * _sigmoid(gate) * up).astype(BF16)
    acc_scr[...] += jnp.dot(act, wo_ref[...], preferred_element_type=F32)

    @pl.when(j == pl.num_programs(1) - 1)
    def _():
        y = x_ref[...] + acc_scr[...]
        if final_norm:
            ms = jnp.mean(y * y, axis=-1, keepdims=True)
            y = (y * lax.rsqrt(ms + EPS)) * fg_ref[...]
        o_ref[...] = y


def _ffn(x2d, g, w_gate, w_up, w_out, fg, *, tm, tf, final_norm):
    n, d = x2d.shape
    dff = w_gate.shape[1]
    return pl.pallas_call(
        functools.partial(_ffn_kernel, final_norm=final_norm),
        grid=(n // tm, dff // tf),
        in_specs=[
            pl.BlockSpec((tm, d), lambda i, j: (i, 0)),
            pl.BlockSpec((1, d), lambda i, j: (0, 0)),
            pl.BlockSpec((d, tf), lambda i, j: (0, j)),
            pl.BlockSpec((d, tf), lambda i, j: (0, j)),
            pl.BlockSpec((tf, d), lambda i, j: (j, 0)),
            pl.BlockSpec((1, d), lambda i, j: (0, 0)),
        ],
        out_specs=pl.BlockSpec((tm, d), lambda i, j: (i, 0)),
        out_shape=jax.ShapeDtypeStruct((n, d), F32),
        scratch_shapes=[pltpu.VMEM((tm, d), BF16), pltpu.VMEM((tm, d), F32)],
        compiler_params=pltpu.CompilerParams(
            dimension_semantics=("parallel", "arbitrary"), vmem_limit_bytes=VMEM_LIMIT_BYTES),
        name="swiglu_ffn",
    )(x2d, g, w_gate, w_up, w_out, fg)


def _pick_tile(n, want):
    t = min(n, want)
    while n % t:
        t //= 2
    return t


def kernel(x, rel_bias, mix_norm, w_in, w_attn_out, dw_kernel, dw_bias, conv_norm_g, conv_norm_b,
           w_conv_out, w_mix_out, ffn_norm, w_ffn_in, w_ffn_out, final_norm):
    b, s, d = x.shape
    depth = w_in.shape[0]
    n = b * s
    conv_ch = d // 2
    d_ff = w_ffn_out.shape[1]
    topk = min(TOPK_MAX, s // 4)
    tile = ATTN_TILE
    nq = s // tile

    sizes = (ATTN_WIDTH, ATTN_WIDTH, ATTN_WIDTH, N_IDX_HEADS * IDX_DIM, IDX_DIM, N_IDX_HEADS,
             2 * conv_ch, d, d)
    offs = np.concatenate([[0], np.cumsum(sizes)])
    o_ki, o_wi, o_u, o_g = offs[4], offs[5], offs[6], offs[7]

    bias = _bias_tiles(rel_bias.astype(F32), tile)

    x2d = x.reshape(n, d).astype(F32)
    for l in range(depth):
        w = w_in[l]
        wa = w[:, :o_ki].astype(BF16)
        w_ki = w[:, o_ki:o_wi]
        w_wi = w[:, o_wi:o_u]
        wkw = jnp.concatenate(
            [w_ki, w_ki, w_wi, jnp.zeros((d, LANES - N_IDX_HEADS), w.dtype)], axis=1).astype(BF16)
        wu = w[:, o_u:o_g].astype(BF16)
        wg = w[:, o_g:].astype(BF16)

        qt, k, vt4, qit, ki2, wi, u, gg = _inproj(
            x2d, mix_norm[l].reshape(1, d).astype(F32), wa, wkw, wu, wg,
            batch=b, tm=_pick_tile(s, 512), tile=tile)

        wit = wi[:, :N_IDX_HEADS].reshape(b, s, N_IDX_HEADS).transpose(0, 2, 1)
        attn = _attention(qt, qit, wit, k.reshape(b, s, -1), vt4,
                          ki2.reshape(b, s, -1), bias, tile=tile, topk=topk)

        conv = _conformer_conv(
            u.reshape(b, s, -1), dw_kernel[l].astype(F32), dw_bias[l].reshape(1, -1).astype(F32),
            conv_norm_g[l].reshape(1, -1).astype(F32), conv_norm_b[l].reshape(1, -1).astype(F32),
            tc=_pick_tile(s, 512))

        x2d = _mix(x2d, attn.reshape(n, -1), conv.reshape(n, -1), gg,
                   w_attn_out[l].astype(BF16), w_conv_out[l].astype(BF16), w_mix_out[l].astype(BF16),
                   tm=_pick_tile(n, 512))

        wf = w_ffn_in[l]
        x2d = _ffn(x2d, ffn_norm[l].reshape(1, d).astype(F32),
                   wf[:, :d_ff].astype(BF16), wf[:, d_ff:].astype(BF16), w_ffn_out[l].astype(BF16),
                   final_norm.reshape(1, d).astype(F32),
                   tm=_pick_tile(n, 1024), tf=256, final_norm=(l == depth - 1))
    return x2d.reshape(b, s, d).astype(x.dtype)
```

```python
import functools
import math

import numpy as np
import jax
import jax.numpy as jnp
from jax import lax
from jax.experimental import pallas as pl
from jax.experimental.pallas import tpu as pltpu

F32 = jnp.float32
BF16 = jnp.bfloat16
I32 = jnp.int32
I16 = jnp.int16

N_HEADS = 8
HEAD_DIM = 64
ATTN_WIDTH = N_HEADS * HEAD_DIM
N_IDX_HEADS = 8
IDX_DIM = 64
TOPK_MAX = 256
CONV_WIDTH = 31
N_BUCKETS = 32
MAX_DISTANCE = 128
EPS = 1e-6

LANES = 128
SUBLANES = 8
BF16_ROWS = 16
VMEM_LIMIT_BYTES = 56 * 1024 * 1024
LOG2E = math.log2(math.e)
NEG_BIG = -1e30
INT_MIN = -(2 ** 31)
I16_MIN = -(2 ** 15)
I16_MAX = 2 ** 15 - 1

ATTN_TILE = 256
V_ROWS = HEAD_DIM + BF16_ROWS
SCAN_GROUP = 8
FAR_GROUP = 4
SCORE_GROUP = 8
CONV_HALO = 32


def _sigmoid(x):
    return 1.0 / (1.0 + jnp.exp(-x))


def _t5_bucket_ranges():
    max_exact = N_BUCKETS // 2
    buckets = []
    for n in range(0, 4 * MAX_DISTANCE):
        if n < max_exact:
            buckets.append(n)
        else:
            b = max_exact + int(math.log(n / max_exact) / math.log(MAX_DISTANCE / max_exact)
                                * (N_BUCKETS - max_exact))
            buckets.append(min(b, N_BUCKETS - 1))
    ranges = []
    for b in range(N_BUCKETS - 1):
        ns = [n for n, bb in enumerate(buckets) if bb == b]
        ranges.append((min(ns), max(ns)))
    return ranges


_BUCKET_RANGES = _t5_bucket_ranges()
_BIAS_REACH = _BUCKET_RANGES[-1][1] + 1


def _inproj_kernel(x_ref, g_ref, wa_ref, wkw_ref, wu_ref, wg_ref,
                   qt_ref, k_ref, vt_ref, qit_ref, ki_ref, wi_ref, u_ref, gg_ref, t_scr,
                   *, q_scale, tile):
    x = x_ref[...]
    tm = x.shape[0]
    ms = jnp.mean(x * x, axis=-1, keepdims=True)
    h = ((x * lax.rsqrt(ms + EPS)) * g_ref[...]).astype(BF16)

    def mm(w):
        return jnp.dot(h, w, preferred_element_type=F32)

    w = ATTN_WIDTH

    def transposed(y):
        t_scr[...] = y
        return t_scr[...].T.astype(BF16)

    qt_ref[0] = transposed(mm(wa_ref[:, 0:w]) * q_scale)
    k_ref[...] = mm(wa_ref[:, w:2 * w]).astype(BF16)
    qit_ref[0] = transposed(mm(wa_ref[:, 3 * w:4 * w]))
    vt = transposed(mm(wa_ref[:, 2 * w:3 * w]))
    tail_row = lax.broadcasted_iota(I32, (V_ROWS - HEAD_DIM, tile), 0)
    tail = jnp.where(tail_row == 0, 1.0, 0.0).astype(BF16)
    for c in range(tm // tile):
        for hd in range(N_HEADS):
            base = V_ROWS * hd
            vt_ref[0, c, base:base + HEAD_DIM, :] = vt[HEAD_DIM * hd:HEAD_DIM * (hd + 1),
                                                       tile * c:tile * (c + 1)]
            vt_ref[0, c, base + HEAD_DIM:base + V_ROWS, :] = tail
    kw = mm(wkw_ref[...])
    ki_ref[...] = kw[:, :LANES].astype(BF16)
    wi_ref[...] = kw[:, LANES:]
    u_ref[...] = mm(wu_ref[...])
    gg_ref[...] = mm(wg_ref[...])


def _inproj(x2d, g, wa, wkw, wu, wg, *, batch, tm, tile):
    n, d = x2d.shape
    seq = n // batch
    assert seq % tm == 0 and tm % tile == 0
    spb = seq // tm
    grid = (n // tm,)
    row = lambda i: (i, 0)
    const = lambda i: (0, 0)
    feat_major = lambda i: (i // spb, 0, i % spb)
    out_shape = (
        jax.ShapeDtypeStruct((batch, ATTN_WIDTH, seq), BF16),
        jax.ShapeDtypeStruct((n, ATTN_WIDTH), BF16),
        jax.ShapeDtypeStruct((batch, seq // tile, N_HEADS * V_ROWS, tile), BF16),
        jax.ShapeDtypeStruct((batch, ATTN_WIDTH, seq), BF16),
        jax.ShapeDtypeStruct((n, LANES), BF16),
        jax.ShapeDtypeStruct((n, LANES), F32),
        jax.ShapeDtypeStruct((n, wu.shape[1]), F32),
        jax.ShapeDtypeStruct((n, wg.shape[1]), F32),
    )
    out_specs = [
        pl.BlockSpec((1, ATTN_WIDTH, tm), feat_major),
        pl.BlockSpec((tm, ATTN_WIDTH), row),
        pl.BlockSpec((1, tm // tile, N_HEADS * V_ROWS, tile), lambda i: (i // spb, i % spb, 0, 0)),
        pl.BlockSpec((1, ATTN_WIDTH, tm), feat_major),
        pl.BlockSpec((tm, LANES), row),
        pl.BlockSpec((tm, LANES), row),
        pl.BlockSpec((tm, wu.shape[1]), row),
        pl.BlockSpec((tm, wg.shape[1]), row),
    ]
    return pl.pallas_call(
        functools.partial(_inproj_kernel, q_scale=HEAD_DIM ** -0.5 * LOG2E, tile=tile),
        grid=grid,
        in_specs=[
            pl.BlockSpec((tm, d), row),
            pl.BlockSpec((1, d), const),
            pl.BlockSpec(wa.shape, const),
            pl.BlockSpec(wkw.shape, const),
            pl.BlockSpec(wu.shape, const),
            pl.BlockSpec(wg.shape, const),
        ],
        out_specs=out_specs,
        out_shape=out_shape,
        scratch_shapes=[pltpu.VMEM((tm, ATTN_WIDTH), F32)],
        compiler_params=pltpu.CompilerParams(
            dimension_semantics=("parallel",), vmem_limit_bytes=VMEM_LIMIT_BYTES),
        name="inproj",
    )(x2d, g, wa, wkw, wu, wg)


def _bias_kernel(tab_ref, out_ref, *, tile):
    r = lax.broadcasted_iota(I32, (tile, tile), 0)
    c = lax.broadcasted_iota(I32, (tile, tile), 1)
    for w in range(2):
        dist = c - r + w * tile
        for h in range(N_HEADS):
            far = tab_ref[N_BUCKETS - 1, h]
            val = jnp.zeros((tile, tile), F32)
            for b, (lo, hi) in enumerate(_BUCKET_RANGES):
                hit = (dist == lo) if lo == hi else ((dist >= lo) & (dist <= hi))
                val = jnp.where(hit, (tab_ref[b, h] - far) * LOG2E, val)
            out_ref[w, h] = val


def _bias_tiles(rel_bias, tile):
    return pl.pallas_call(
        functools.partial(_bias_kernel, tile=tile),
        in_specs=[pl.BlockSpec(memory_space=pltpu.SMEM)],
        out_specs=pl.BlockSpec(memory_space=pltpu.VMEM),
        out_shape=jax.ShapeDtypeStruct((2, N_HEADS, tile, tile), F32),
        name="bias_tiles",
    )(rel_bias)


def _order_key(x):
    return x ^ ((x >> 31) & 0x7FFFFFFF)


def _attn_kernel(qt_ref, qit_ref, wi_ref, k_ref, vt_ref, ki_ref, bias_ref, o_ref,
                 hi_scr, lo_scr, mask_scr, s_scr, acc_scr, m_scr, alpha_scr, tot_scr, col_scr,
                 qpad_scr, qipad_scr, *, tile, topk, tile_bits, wi_scale):
    t = tile
    i = pl.program_id(1)
    n_chunks = i + 1
    n_groups = n_chunks // SCAN_GROUP

    def fold(x, op):
        return op(x.reshape(t // SUBLANES, SUBLANES, t), axis=0)

    def rows16(r):
        return slice(BF16_ROWS * r, BF16_ROWS * (r + 1))

    def to16(x):
        return jnp.broadcast_to(x, (BF16_ROWS, t)).astype(I16)

    feat = lax.broadcasted_iota(I32, (LANES, t), 0)
    for h in range(N_HEADS):
        pair = slice(LANES * (h // 2), LANES * (h // 2) + LANES)
        lo = HEAD_DIM * (h % 2)
        in_head = (feat >= lo) & (feat < lo + HEAD_DIM)
        qpad_scr[h] = jnp.where(in_head, qt_ref[0, pair, :], 0).astype(BF16)
        qipad_scr[h] = jnp.where(in_head, qit_ref[0, pair, :], 0).astype(BF16)

    w_all = wi_ref[0] * wi_scale
    row = lax.broadcasted_iota(I32, (t, t), 0)
    col = lax.broadcasted_iota(I32, (t, t), 1)

    def score_chunk(j, slot, diagonal):
        kij = ki_ref[0, pl.ds(pl.multiple_of(j * t, t), t), :]
        for hp in range(N_IDX_HEADS // 2):
            ha, hb = 2 * hp, 2 * hp + 1
            la = jnp.dot(kij, qipad_scr[ha], preferred_element_type=F32)
            lb = jnp.dot(kij, qipad_scr[hb], preferred_element_type=F32)
            term = (jnp.maximum(la, 0.0) * w_all[ha:ha + 1, :]
                    + jnp.maximum(lb, 0.0) * w_all[hb:hb + 1, :])
            if hp == 0:
                s_scr[slot] = term
            elif hp < N_IDX_HEADS // 2 - 1:
                s_scr[slot] = s_scr[slot] + term
            else:
                score = s_scr[slot] + term
                score = jnp.where(score == 0.0, 0.0, score)
                skey = _order_key(pltpu.bitcast(score, I32))
                if diagonal:
                    skey = jnp.where(row <= col, skey, INT_MIN)
                hi_scr[j] = (skey >> 16).astype(I16)
                lo_scr[j] = ((skey & 0xFFFF) + I16_MIN).astype(I16)

    def score_group(size, first):
        def body(g, carry):
            for u in range(size):
                score_chunk(first + g * size + u, u, False)
            return carry
        return body

    n_big = i // SCORE_GROUP
    n_twos = (i - SCORE_GROUP * n_big) // 2
    lax.fori_loop(0, n_big, score_group(SCORE_GROUP, 0), 0)
    lax.fori_loop(0, n_twos, score_group(2, SCORE_GROUP * n_big), 0)

    @pl.when(i % 2 == 1)
    def _():
        score_chunk(i - 1, 0, False)

    score_chunk(i, 1, True)

    def count_ge16(plane, cands):
        cands16 = [to16(c) for c in cands]

        def scan_chunk(j, parts):
            parts = list(parts)
            for ci, c16 in enumerate(cands16):
                acc = parts[ci]
                for r in range(t // BF16_ROWS):
                    acc = acc + jnp.where(plane[j, rows16(r), :] >= c16, jnp.int16(1), jnp.int16(0))
                parts[ci] = acc
            return tuple(parts)

        def scan_group(size, first):
            def body(g, parts):
                for u in range(size):
                    parts = scan_chunk(first + g * size + u, parts)
                return parts
            return body

        zero = jnp.zeros((BF16_ROWS, t), I16)
        half = SCAN_GROUP // 2
        done = n_groups * SCAN_GROUP
        n_half = (n_chunks - done) // half
        parts = lax.fori_loop(0, n_groups, scan_group(SCAN_GROUP, 0), tuple(zero for _ in cands))
        parts = lax.fori_loop(0, n_half, scan_group(half, done), parts)
        parts = lax.fori_loop(done + n_half * half, n_chunks, scan_chunk, parts)
        return [jnp.sum(p.astype(I32), axis=0, keepdims=True) for p in parts]

    def select16(plane, want):
        def bit_step(it, state):
            cur_u, c_ge, c_gt = state
            cand_u = cur_u | (jnp.int32(1) << (15 - it))
            cnt, = count_ge16(plane, [cand_u + I16_MIN])
            ok = cnt >= want
            return jnp.where(ok, cand_u, cur_u), jnp.where(ok, cnt, c_ge), jnp.where(ok, c_gt, cnt)

        everything = jnp.full((1, t), n_chunks * t, I32)
        cur_u, c_ge, c_gt = lax.fori_loop(
            0, 16, bit_step, (jnp.zeros((1, t), I32), everything, jnp.zeros((1, t), I32)))
        return cur_u + I16_MIN, c_ge, c_gt

    th, c_hi_ge, c_above = select16(hi_scr, jnp.full((1, t), topk, I32))
    th16 = to16(th)

    def low_candidates(j, carry):
        for r in range(t // BF16_ROWS):
            lo_scr[j, rows16(r), :] = jnp.where(hi_scr[j, rows16(r), :] == th16,
                                                lo_scr[j, rows16(r), :], jnp.int16(I16_MIN))
        return carry

    lax.fori_loop(0, n_chunks, low_candidates, 0)

    tl, c_lo_ge, c_lo_gt = select16(lo_scr, topk - c_above)
    c_ge = jnp.where(tl == I16_MIN, c_hi_ge, c_above + c_lo_ge)
    need = topk - (c_above + c_lo_gt)
    has_thr = th > I16_MIN
    tie_split = (c_ge > topk) & has_thr
    any_split = jnp.max(jnp.where(tie_split, 1, 0)) > 0

    tl16 = to16(tl)
    keep = jnp.zeros((BF16_ROWS, t), BF16)
    drop = jnp.full((BF16_ROWS, t), NEG_BIG, BF16)
    one16 = jnp.ones((BF16_ROWS, t), I16)
    zero16 = jnp.zeros((BF16_ROWS, t), I16)

    def lanes16(pred):
        return to16(jnp.where(pred, 1, 0)) > 0

    @pl.when(jnp.logical_not(any_split))
    def _():
        tie = jnp.where(lanes16(has_thr), keep, drop)

        def mask_chunk(j, carry):
            for r in range(t // BF16_ROWS):
                hs = hi_scr[j, rows16(r), :]
                ls = lo_scr[j, rows16(r), :]
                at_th = jnp.where(ls > tl16, keep, jnp.where(ls == tl16, tie, drop))
                mb = jnp.where(hs > th16, keep, jnp.where(hs == th16, at_th, drop))
                mask_scr[j, rows16(r), :] = mb.astype(F32)
            return carry

        lax.fori_loop(0, n_chunks, mask_chunk, 0)

    @pl.when(any_split)
    def _():
        row16 = lax.broadcasted_iota(I32, (BF16_ROWS, t), 0).astype(I16)

        def tie_tile(j, r):
            hs = hi_scr[j, rows16(r), :]
            ls = lo_scr[j, rows16(r), :]
            return jnp.where(hs == th16, jnp.where(ls == tl16, one16, zero16), zero16)

        def chunk_total(j, carry):
            acc = zero16
            for r in range(t // BF16_ROWS):
                acc = acc + tie_tile(j, r)
            tot = jnp.sum(acc.astype(I32), axis=0, keepdims=True)
            tot_scr[j] = jnp.broadcast_to(tot, (SUBLANES, t))
            return carry

        lax.fori_loop(0, n_chunks, chunk_total, 0)

        def locate_chunk(j, carry):
            cum, n_before, ties_before = carry
            tot = tot_scr[j][0:1, :]
            cum = cum + tot
            before = cum < need
            return cum, n_before + jnp.where(before, 1, 0), ties_before + jnp.where(before, tot, 0)

        zero1 = jnp.zeros((1, t), I32)
        _, cut_chunk, ties_before = lax.fori_loop(0, n_chunks, locate_chunk, (zero1, zero1, zero1))

        col_scr[...] = jnp.zeros(col_scr.shape, I16)

        def gather_chunk(j, carry):
            here = lanes16(cut_chunk == j)
            for r in range(t // BF16_ROWS):
                col_scr[rows16(r), :] = jnp.where(here, tie_tile(j, r), col_scr[rows16(r), :])
            return carry

        lax.fori_loop(0, n_chunks, gather_chunk, 0)

        wanted = need - ties_before

        def row_bit(it, cur):
            cand = cur | (jnp.int32(1) << (tile_bits - 1 - it))
            cand16 = to16(cand)
            acc = zero16
            for r in range(t // BF16_ROWS):
                acc = acc + jnp.where(row16 + jnp.int16(BF16_ROWS * r) < cand16,
                                      col_scr[rows16(r), :], zero16)
            cnt = jnp.sum(acc.astype(I32), axis=0, keepdims=True)
            return jnp.where(cnt < wanted, cand, cur)

        cut_row16 = to16(lax.fori_loop(0, tile_bits, row_bit, zero1))
        cut_chunk = jnp.where(tie_split, cut_chunk, jnp.where(has_thr, jnp.int32(2 ** 30), -1))

        def mask_chunk(j, carry):
            whole = lanes16(j < cut_chunk)
            partial = lanes16(j == cut_chunk)
            for r in range(t // BF16_ROWS):
                hs = hi_scr[j, rows16(r), :]
                ls = lo_scr[j, rows16(r), :]
                in_cut = jnp.where(row16 + jnp.int16(BF16_ROWS * r) <= cut_row16, keep, drop)
                tie = jnp.where(whole, keep, jnp.where(partial, in_cut, drop))
                at_th = jnp.where(ls > tl16, keep, jnp.where(ls == tl16, tie, drop))
                mb = jnp.where(hs > th16, keep, jnp.where(hs == th16, at_th, drop))
                mask_scr[j, rows16(r), :] = mb.astype(F32)
            return carry

        lax.fori_loop(0, n_chunks, mask_chunk, 0)

    m_scr[...] = jnp.full(m_scr.shape, -jnp.inf, F32)
    acc_scr[...] = jnp.zeros(acc_scr.shape, F32)

    def attend(j, slot0, bias_sel, live=None):
        for h in range(N_HEADS):
            pair = slice(LANES * (h // 2), LANES * (h // 2) + LANES)
            kc = k_ref[0, pl.ds(pl.multiple_of(j * t, t), t), pair]
            mb = mask_scr[j]
            if live is not None:
                mb = jnp.where(live, mb, NEG_BIG)
            s = jnp.dot(kc, qpad_scr[h], preferred_element_type=F32) + mb
            if bias_sel is not None:
                s = s + bias_ref[bias_sel, h]
            s_scr[slot0 + h] = s
            m_old = m_scr[h:h + 1, :]
            m_new = jnp.maximum(m_old, jnp.max(s, axis=0, keepdims=True))
            alpha_scr[slot0 + h:slot0 + h + 1, :] = jnp.exp2(m_old - m_new)
            m_scr[h:h + 1, :] = m_new
        for h in range(N_HEADS):
            rows = slice(V_ROWS * h, V_ROWS * (h + 1))
            p = jnp.exp2((s_scr[slot0 + h] - m_scr[h:h + 1, :]).astype(BF16))
            pv = jnp.dot(vt_ref[0, j, rows, :], p, preferred_element_type=F32)
            acc_scr[rows, :] = alpha_scr[slot0 + h:slot0 + h + 1, :] * acc_scr[rows, :] + pv

    n_far = jnp.maximum(i - 1, 0)

    def far_group(size, first):
        def body(g, carry):
            for u in range(size):
                attend(first + g * size + u, N_HEADS * u, None)
            return carry
        return body

    n_big = n_far // FAR_GROUP
    lax.fori_loop(0, n_big, far_group(FAR_GROUP, 0), 0)
    lax.fori_loop(0, (n_far - FAR_GROUP * n_big) // 2, far_group(2, FAR_GROUP * n_big), 0)

    @pl.when(n_far % 2 == 1)
    def _():
        attend(n_far - 1, 0, None)

    attend(jnp.maximum(i - 1, 0), 0, 1, live=i >= 1)
    attend(i, N_HEADS, 0)

    outs = []
    for h in range(N_HEADS):
        base = V_ROWS * h
        inv_l = 1.0 / acc_scr[base + HEAD_DIM:base + HEAD_DIM + 1, :]
        outs.append(acc_scr[base:base + HEAD_DIM, :] * inv_l)
    o_ref[0] = jnp.concatenate(outs, axis=0).T.astype(BF16)


def _attention(qt, qit, wit, k, vt4, ki2, bias, *, tile, topk):
    b, s, _ = k.shape
    nq = s // tile
    assert tile >= topk and tile >= _BIAS_REACH and s % tile == 0 and (tile & (tile - 1)) == 0
    kern = functools.partial(_attn_kernel, tile=tile, topk=topk, tile_bits=tile.bit_length() - 1,
                             wi_scale=(N_IDX_HEADS ** -0.5) * (IDX_DIM ** -0.5))
    resident = pl.Buffered(1)
    return pl.pallas_call(
        kern,
        grid=(b, nq),
        in_specs=[
            pl.BlockSpec((1, ATTN_WIDTH, tile), lambda bb, i: (bb, 0, i)),
            pl.BlockSpec((1, ATTN_WIDTH, tile), lambda bb, i: (bb, 0, i)),
            pl.BlockSpec((1, N_IDX_HEADS, tile), lambda bb, i: (bb, 0, i)),
            pl.BlockSpec((1, s, ATTN_WIDTH), lambda bb, i: (bb, 0, 0), pipeline_mode=resident),
            pl.BlockSpec((1, nq, N_HEADS * V_ROWS, tile), lambda bb, i: (bb, 0, 0, 0),
                         pipeline_mode=resident),
            pl.BlockSpec((1, s, LANES), lambda bb, i: (bb, 0, 0), pipeline_mode=resident),
            pl.BlockSpec((2, N_HEADS, tile, tile), lambda bb, i: (0, 0, 0, 0),
                         pipeline_mode=resident),
        ],
        out_specs=pl.BlockSpec((1, tile, ATTN_WIDTH), lambda bb, i: (bb, i, 0)),
        out_shape=jax.ShapeDtypeStruct((b, s, ATTN_WIDTH), BF16),
        scratch_shapes=[
            pltpu.VMEM((nq, tile, tile), I16),
            pltpu.VMEM((nq, tile, tile), I16),
            pltpu.VMEM((nq, tile, tile), F32),
            pltpu.VMEM((FAR_GROUP * N_HEADS, tile, tile), F32),
            pltpu.VMEM((N_HEADS * V_ROWS, tile), F32),
            pltpu.VMEM((N_HEADS, tile), F32),
            pltpu.VMEM((FAR_GROUP * N_HEADS, tile), F32),
            pltpu.VMEM((nq, SUBLANES, tile), I32),
            pltpu.VMEM((tile, tile), I16),
            pltpu.VMEM((N_HEADS, LANES, tile), BF16),
            pltpu.VMEM((N_IDX_HEADS, LANES, tile), BF16),
        ],
        compiler_params=pltpu.CompilerParams(
            dimension_semantics=("arbitrary", "arbitrary"), vmem_limit_bytes=VMEM_LIMIT_BYTES),
        name="dsa_attention",
    )(qt, qit, wit, k, vt4, ki2, bias)


def _conv_kernel(u_ref, dw_ref, db_ref, g_ref, b_ref, o_ref, h_scr, shift_scr, *, tc, sub):
    ch = o_ref.shape[-1]
    halo = CONV_HALO

    @pl.when(pl.program_id(1) == 0)
    def _():
        h_scr[0:halo, :] = jnp.zeros((halo, ch), F32)

    u = u_ref[0]
    h_scr[halo:halo + tc, :] = u[:, :ch] * _sigmoid(u[:, ch:])
    first = halo - (CONV_WIDTH - 1)

    span = tc + halo - SUBLANES
    for s in range(1, SUBLANES):
        shift_scr[s - 1, 0:span, :] = h_scr[s:s + span, :]

    for r in range(tc // sub):
        base = r * sub
        acc = jnp.zeros((sub, ch), F32)
        for j in range(CONV_WIDTH):
            s = (first + j) % SUBLANES
            lo = base + first + j - s
            rows = h_scr[lo:lo + sub, :] if s == 0 else shift_scr[s - 1, lo:lo + sub, :]
            acc = acc + dw_ref[j:j + 1, :] * rows
        acc = acc + db_ref[...]
        mu = jnp.mean(acc, axis=-1, keepdims=True)
        cen = acc - mu
        var = jnp.mean(cen * cen, axis=-1, keepdims=True)
        y = cen * lax.rsqrt(var + EPS) * g_ref[...] + b_ref[...]
        o_ref[0, base:base + sub, :] = (y * _sigmoid(y)).astype(o_ref.dtype)
    h_scr[0:halo, :] = h_scr[tc:tc + halo, :]


def _conformer_conv(u, dw, db, g, bb, *, tc, sub=64):
    b, s, c2 = u.shape
    ch = c2 // 2
    const = lambda bi, i: (0, 0)
    return pl.pallas_call(
        functools.partial(_conv_kernel, tc=tc, sub=sub),
        grid=(b, s // tc),
        in_specs=[
            pl.BlockSpec((1, tc, c2), lambda bi, i: (bi, i, 0)),
            pl.BlockSpec(dw.shape, const),
            pl.BlockSpec((1, ch), const),
            pl.BlockSpec((1, ch), const),
            pl.BlockSpec((1, ch), const),
        ],
        out_specs=pl.BlockSpec((1, tc, ch), lambda bi, i: (bi, i, 0)),
        out_shape=jax.ShapeDtypeStruct((b, s, ch), BF16),
        scratch_shapes=[pltpu.VMEM((tc + CONV_HALO, ch), F32),
                        pltpu.VMEM((SUBLANES - 1, tc + CONV_HALO, ch), F32)],
        compiler_params=pltpu.CompilerParams(
            dimension_semantics=("arbitrary", "arbitrary"), vmem_limit_bytes=VMEM_LIMIT_BYTES),
        name="conformer_conv",
    )(u, dw, db, g, bb)


def _mix_kernel(x_ref, a_ref, c_ref, gg_ref, wa_ref, wc_ref, wm_ref, o_ref):
    d = x_ref.shape[-1]
    y_a = jnp.dot(a_ref[...], wa_ref[...], preferred_element_type=F32)
    y_b = jnp.dot(c_ref[...], wc_ref[...], preferred_element_type=F32)
    gg = gg_ref[...]
    merged = _sigmoid(gg[:, :d]) * y_a + _sigmoid(gg[:, d:]) * y_b
    o_ref[...] = x_ref[...] + jnp.dot(merged.astype(BF16), wm_ref[...], preferred_element_type=F32)


def _mix(x2d, attn, conv, gg, wa, wc, wm, *, tm):
    n, d = x2d.shape
    row = lambda i: (i, 0)
    const = lambda i: (0, 0)
    return pl.pallas_call(
        _mix_kernel,
        grid=(n // tm,),
        in_specs=[
            pl.BlockSpec((tm, d), row),
            pl.BlockSpec((tm, attn.shape[1]), row),
            pl.BlockSpec((tm, conv.shape[1]), row),
            pl.BlockSpec((tm, gg.shape[1]), row),
            pl.BlockSpec(wa.shape, const),
            pl.BlockSpec(wc.shape, const),
            pl.BlockSpec(wm.shape, const),
        ],
        out_specs=pl.BlockSpec((tm, d), row),
        out_shape=jax.ShapeDtypeStruct((n, d), F32),
        compiler_params=pltpu.CompilerParams(
            dimension_semantics=("parallel",), vmem_limit_bytes=VMEM_LIMIT_BYTES),
        name="gated_mix",
    )(x2d, attn, conv, gg, wa, wc, wm)


def _ffn_kernel(x_ref, g_ref, wg_ref, wu_ref, wo_ref, fg_ref, o_ref, h_scr, acc_scr, *, final_norm):
    j = pl.program_id(1)

    @pl.when(j == 0)
    def _():
        x = x_ref[...]
        ms = jnp.mean(x * x, axis=-1, keepdims=True)
        h_scr[...] = ((x * lax.rsqrt(ms + EPS)) * g_ref[...]).astype(BF16)
        acc_scr[...] = jnp.zeros(acc_scr.shape, F32)

    h = h_scr[...]
    gate = jnp.dot(h, wg_ref[...], preferred_element_type=F32)
    up = jnp.dot(h, wu_ref[...], preferred_element_type=F32)
    act = (gate * _sigmoid(gate) * up).astype(BF16)
    acc_scr[...] += jnp.dot(act, wo_ref[...], preferred_element_type=F32)

    @pl.when(j == pl.num_programs(1) - 1)
    def _():
        y = x_ref[...] + acc_scr[...]
        if final_norm:
            ms = jnp.mean(y * y, axis=-1, keepdims=True)
            y = (y * lax.rsqrt(ms + EPS)) * fg_ref[...]
        o_ref[...] = y


def _ffn(x2d, g, w_gate, w_up, w_out, fg, *, tm, tf, final_norm):
    n, d = x2d.shape
    dff = w_gate.shape[1]
    return pl.pallas_call(
        functools.partial(_ffn_kernel, final_norm=final_norm),
        grid=(n // tm, dff // tf),
        in_specs=[
            pl.BlockSpec((tm, d), lambda i, j: (i, 0)),
            pl.BlockSpec((1, d), lambda i, j: (0, 0)),
            pl.BlockSpec((d, tf), lambda i, j: (0, j)),
            pl.BlockSpec((d, tf), lambda i, j: (0, j)),
            pl.BlockSpec((tf, d), lambda i, j: (j, 0)),
            pl.BlockSpec((1, d), lambda i, j: (0, 0)),
        ],
        out_specs=pl.BlockSpec((tm, d), lambda i, j: (i, 0)),
        out_shape=jax.ShapeDtypeStruct((n, d), F32),
        scratch_shapes=[pltpu.VMEM((tm, d), BF16), pltpu.VMEM((tm, d), F32)],
        compiler_params=pltpu.CompilerParams(
            dimension_semantics=("parallel", "arbitrary"), vmem_limit_bytes=VMEM_LIMIT_BYTES),
        name="swiglu_ffn",
    )(x2d, g, w_gate, w_up, w_out, fg)


def _pick_tile(n, want):
    t = min(n, want)
    while n % t:
        t //= 2
    return t


def kernel(x, rel_bias, mix_norm, w_in, w_attn_out, dw_kernel, dw_bias, conv_norm_g, conv_norm_b,
           w_conv_out, w_mix_out, ffn_norm, w_ffn_in, w_ffn_out, final_norm):
    b, s, d = x.shape
    depth = w_in.shape[0]
    n = b * s
    conv_ch = d // 2
    d_ff = w_ffn_out.shape[1]
    topk = min(TOPK_MAX, s // 4)
    tile = ATTN_TILE

    sizes = (ATTN_WIDTH, ATTN_WIDTH, ATTN_WIDTH, N_IDX_HEADS * IDX_DIM, IDX_DIM, N_IDX_HEADS,
             2 * conv_ch, d, d)
    offs = np.concatenate([[0], np.cumsum(sizes)])
    o_ki, o_wi, o_u, o_g = offs[4], offs[5], offs[6], offs[7]

    bias = _bias_tiles(rel_bias.astype(F32), tile)

    x2d = x.reshape(n, d).astype(F32)
    for l in range(depth):
        w = w_in[l]
        wa = w[:, :o_ki].astype(BF16)
        w_ki = w[:, o_ki:o_wi]
        w_wi = w[:, o_wi:o_u]
        wkw = jnp.concatenate(
            [w_ki, w_ki, w_wi, jnp.zeros((d, LANES - N_IDX_HEADS), w.dtype)], axis=1).astype(BF16)
        wu = w[:, o_u:o_g].astype(BF16)
        wg = w[:, o_g:].astype(BF16)

        qt, k, vt4, qit, ki2, wi, u, gg = _inproj(
            x2d, mix_norm[l].reshape(1, d).astype(F32), wa, wkw, wu, wg,
            batch=b, tm=_pick_tile(s, 512), tile=tile)

        wit = wi[:, :N_IDX_HEADS].reshape(b, s, N_IDX_HEADS).transpose(0, 2, 1)
        attn = _attention(qt, qit, wit, k.reshape(b, s, -1), vt4,
                          ki2.reshape(b, s, -1), bias, tile=tile, topk=topk)

        conv = _conformer_conv(
            u.reshape(b, s, -1), dw_kernel[l].astype(F32), dw_bias[l].reshape(1, -1).astype(F32),
            conv_norm_g[l].reshape(1, -1).astype(F32), conv_norm_b[l].reshape(1, -1).astype(F32),
            tc=_pick_tile(s, 512))

        x2d = _mix(x2d, attn.reshape(n, -1), conv.reshape(n, -1), gg,
                   w_attn_out[l].astype(BF16), w_conv_out[l].astype(BF16), w_mix_out[l].astype(BF16),
                   tm=_pick_tile(n, 512))

        wf = w_ffn_in[l]
        x2d = _ffn(x2d, ffn_norm[l].reshape(1, d).astype(F32),
                   wf[:, :d_ff].astype(BF16), wf[:, d_ff:].astype(BF16), w_ffn_out[l].astype(BF16),
                   final_norm.reshape(1, d).astype(F32),
                   tm=_pick_tile(n, 1024), tf=256, final_norm=(l == depth - 1))
    return x2d.reshape(b, s, d).astype(x.dtype)
```
